```python
import jax, jax.numpy as jnp
from jax import lax
import numpy as np

D_MODEL = 1024
BATCH = 32
SEQ = 2048
DEPTH = 2

N_MIXERS = 2
N_MEM = 256
HG_HEADS = 8
HG_DK = D_MODEL // HG_HEADS
HG_DV = D_MODEL // HG_HEADS
D_MIX = HG_HEADS * HG_DV
HG_CHUNK = 16
CONV_WIDTH = 3
XA_HEADS = 4
XA_DH = 128
D_XA = XA_HEADS * XA_DH
D_IN = 4 * D_MIX + D_XA
D_CAT = D_MIX + D_XA
N_HGRN = (DEPTH + 1) // 2
N_CONV = DEPTH // 2
DN_ALPHA = (2 * DEPTH) ** 0.25
DN_BETA = (8 * DEPTH) ** -0.25
LN_EPS = 1e-5
RMS_EPS = 1e-5

kernel_name = "hgrn2_shortconv_memxattn_deepnorm_hybrid"


def layer_norm(x, g, b):
    xf = x.astype(jnp.float32)
    mu = jnp.mean(xf, axis=-1, keepdims=True)
    var = jnp.mean(jnp.square(xf - mu), axis=-1, keepdims=True)
    y = (xf - mu) * lax.rsqrt(var + LN_EPS) * g.astype(jnp.float32) + b.astype(jnp.float32)
    return y.astype(x.dtype)


def hgrn2_mixer(zq, zf, zi, zg, lb, norm_w):
    f32 = jnp.float32
    bsz, slen, _ = zq.shape
    dt = zq.dtype
    q = jax.nn.silu(zq.astype(f32))
    f = lb + (1.0 - lb) * jax.nn.sigmoid(zf.astype(f32))
    k = 1.0 - f
    logf = jnp.log(f)
    v = zi.astype(f32)
    nc = slen // HG_CHUNK

    def to_chunks(t, d):
        return t.reshape(bsz, nc, HG_CHUNK, HG_HEADS, d).transpose(1, 0, 3, 2, 4)

    qc, kc, vc, gc = to_chunks(q, HG_DK), to_chunks(k, HG_DK), to_chunks(v, HG_DV), to_chunks(logf, HG_DK)
    causal = jnp.tril(jnp.ones((HG_CHUNK, HG_CHUNK), dtype=bool))[:, :, None]

    def step(state, inp):
        qq, kk, vv, gg = inp
        b = jnp.cumsum(gg, axis=2)
        o_inter = jnp.einsum('bhtk,bhkv->bhtv', qq * jnp.exp(b), state)
        diff = b[:, :, :, None, :] - b[:, :, None, :, :]
        decay = jnp.exp(jnp.where(causal, diff, -jnp.inf))
        scores = jnp.einsum('bhtk,bhsk,bhtsk->bhts', qq, kk, decay)
        o_intra = jnp.einsum('bhts,bhsv->bhtv', scores, vv)
        b_last = b[:, :, -1, :]
        k_dec = kk * jnp.exp(b_last[:, :, None, :] - b)
        new_state = jnp.exp(b_last)[..., None] * state + jnp.einsum('bhsk,bhsv->bhkv', k_dec, vv)
        return new_state, o_inter + o_intra

    s0 = jnp.zeros((bsz, HG_HEADS, HG_DK, HG_DV), f32)
    _, o = lax.scan(step, s0, (qc, kc, vc, gc))
    o = o.transpose(1, 0, 3, 2, 4).reshape(bsz, slen, HG_HEADS, HG_DV)
    o = o * lax.rsqrt(jnp.mean(jnp.square(o), axis=-1, keepdims=True) + RMS_EPS) * norm_w.astype(f32)
    o = o.reshape(bsz, slen, D_MIX) * jax.nn.silu(zg.astype(f32))
    return o.astype(dt)


def short_conv_mixer(zb, zc, zh, zg, conv_w):
    u = zc * zh
    w = conv_w[:, None, :].astype(u.dtype)
    c = lax.conv_general_dilated(u, w, window_strides=(1,), padding=[(CONV_WIDTH - 1, 0)],
                                 dimension_numbers=('NWC', 'WIO', 'NWC'),
                                 feature_group_count=D_MIX)
    return zb * c * jax.nn.silu(zg)


def mem_cross_attn(zq, mem_k, mem_v):
    bsz, slen, _ = zq.shape
    q = zq.reshape(bsz, slen, XA_HEADS, XA_DH)
    s = jnp.einsum('bshd,bmhd->bhsm', q, mem_k).astype(jnp.float32) * (XA_DH ** -0.5)
    p = jax.nn.softmax(s, axis=-1).astype(zq.dtype)
    o = jnp.einsum('bhsm,bmhd->bshd', p, mem_v)
    return o.reshape(bsz, slen, D_XA)


def setup_inputs(seed: int = 0) -> dict:
    key = jax.random.key(seed)
    ks = jax.random.split(key, 14)
    f32 = jnp.float32
    x = jax.random.normal(ks[0], (BATCH, SEQ, D_MODEL), f32)
    mem = jax.random.normal(ks[1], (BATCH, N_MEM, D_MODEL), f32)
    w_in = jax.random.normal(ks[2], (DEPTH, D_MODEL, D_IN), f32) * D_MODEL ** -0.5
    w_out = jax.random.normal(ks[3], (DEPTH, D_CAT, D_MODEL), f32) * (D_CAT ** -0.5 * DN_BETA)
    ln_g = 1.0 + 0.02 * jax.random.normal(ks[4], (DEPTH, D_MODEL), f32)
    ln_b = 0.02 * jax.random.normal(ks[5], (DEPTH, D_MODEL), f32)
    hgrn_lb_logits = 0.1 * jax.random.normal(ks[6], (N_HGRN + 1, D_MIX), f32)
    hgrn_norm_w = 1.0 + 0.02 * jax.random.normal(ks[7], (N_HGRN, HG_DV), f32)
    conv_w = jax.random.normal(ks[8], (N_CONV, CONV_WIDTH, D_MIX), f32) * CONV_WIDTH ** -0.5
    mem_ln_g = 1.0 + 0.02 * jax.random.normal(ks[9], (D_MODEL,), f32)
    mem_ln_b = 0.02 * jax.random.normal(ks[10], (D_MODEL,), f32)
    w_mem_kv = jax.random.normal(ks[11], (D_MODEL, 2 * D_XA), f32) * D_MODEL ** -0.5
    return {"x": x, "mem": mem, "w_in": w_in, "w_out": w_out, "ln_g": ln_g, "ln_b": ln_b,
            "hgrn_lb_logits": hgrn_lb_logits, "hgrn_norm_w": hgrn_norm_w, "conv_w": conv_w,
            "mem_ln_g": mem_ln_g, "mem_ln_b": mem_ln_b, "w_mem_kv": w_mem_kv}


def reference(x, mem, w_in, w_out, ln_g, ln_b, hgrn_lb_logits, hgrn_norm_w, conv_w,
              mem_ln_g, mem_ln_b, w_mem_kv):
    bsz = x.shape[0]
    kv = layer_norm(mem, mem_ln_g, mem_ln_b) @ w_mem_kv
    mem_k = kv[..., :D_XA].reshape(bsz, N_MEM, XA_HEADS, XA_DH)
    mem_v = kv[..., D_XA:].reshape(bsz, N_MEM, XA_HEADS, XA_DH)
    lb_all = jnp.cumsum(jax.nn.softmax(hgrn_lb_logits.astype(jnp.float32), axis=0), axis=0)[:N_HGRN]
    for layer in range(DEPTH):
        z = x @ w_in[layer]
        za, zb, zc, zg, zx = jnp.split(z, [D_MIX, 2 * D_MIX, 3 * D_MIX, 4 * D_MIX], axis=-1)
        j = layer // N_MIXERS
        if layer % N_MIXERS == 0:
            mix = hgrn2_mixer(za, zb, zc, zg, lb_all[j], hgrn_norm_w[j])
        else:
            mix = short_conv_mixer(za, zb, zc, zg, conv_w[j])
        xa = mem_cross_attn(zx, mem_k, mem_v)
        y = jnp.concatenate([mix, xa], axis=-1) @ w_out[layer]
        x = layer_norm(DN_ALPHA * x + y, ln_g[layer], ln_b[layer])
    return x
```

```python
import functools
import math

import jax
import jax.numpy as jnp
from jax import lax
from jax.experimental import pallas as pl
from jax.experimental.pallas import tpu as pltpu

D_MODEL = 1024
N_MEM = 256
HG_HEADS = 8
HG_DK = 128
D_MIX = 1024
CONV_WIDTH = 3
XA_HEADS = 4
XA_DH = 128
D_XA = XA_HEADS * XA_DH
D_IN = 4 * D_MIX + D_XA
D_CAT = D_MIX + D_XA
DEPTH = 2
DN_ALPHA = (2 * DEPTH) ** 0.25
LN_EPS = 1e-5
RMS_EPS = 1e-5

TOKEN_TILE = 512
HG_CHUNK = 64
IN_PROJ_COLS = 512
VMEM_LIMIT_BYTES = 56 * 1024 * 1024

F32 = jnp.float32
BF16 = jnp.bfloat16
LOG2E = math.log2(math.e)


def _layer_norm_rows(v, g, b):
    mu = jnp.mean(v, axis=-1, keepdims=True)
    d = v - mu
    var = jnp.mean(d * d, axis=-1, keepdims=True)
    return d * lax.rsqrt(var + LN_EPS) * g + b


def _sigmoid(v):
    return 1.0 / (1.0 + jnp.exp(-v))


def _dot(a, b):
    return jnp.dot(a, b, preferred_element_type=F32)


def _dot_nt(a, b):
    return lax.dot_general(a, b, (((1,), (1,)), ((), ())), preferred_element_type=F32)


def _dot_tn(a, b):
    return lax.dot_general(a, b, (((0,), (0,)), ((), ())), preferred_element_type=F32)


def _mem_kv_kernel(mem_ref, w_ref, g_ref, b_ref, kt_ref, v_ref):
    m = _layer_norm_rows(mem_ref[0], g_ref[...], b_ref[...])
    kv = _dot(m.astype(BF16), w_ref[...])
    kt_ref[0] = kv[:, :D_XA].T.astype(BF16)
    v_ref[0] = kv[:, D_XA:].astype(BF16)


def _mem_kv(mem, w_mem_kv, g, b):
    bsz = mem.shape[0]
    return pl.pallas_call(
        _mem_kv_kernel,
        grid=(bsz,),
        in_specs=[
            pl.BlockSpec((1, N_MEM, D_MODEL), lambda i: (i, 0, 0)),
            pl.BlockSpec((D_MODEL, 2 * D_XA), lambda i: (0, 0)),
            pl.BlockSpec((1, D_MODEL), lambda i: (0, 0)),
            pl.BlockSpec((1, D_MODEL), lambda i: (0, 0)),
        ],
        out_specs=[
            pl.BlockSpec((1, D_XA, N_MEM), lambda i: (i, 0, 0)),
            pl.BlockSpec((1, N_MEM, D_XA), lambda i: (i, 0, 0)),
        ],
        out_shape=[
            jax.ShapeDtypeStruct((bsz, D_XA, N_MEM), BF16),
            jax.ShapeDtypeStruct((bsz, N_MEM, D_XA), BF16),
        ],
        compiler_params=pltpu.CompilerParams(
            dimension_semantics=("arbitrary",), vmem_limit_bytes=VMEM_LIMIT_BYTES),
        name="mem_kv",
    )(mem, w_mem_kv, g, b)


def _in_proj(x_ref, w_in_ref, z_ref):
    xb = x_ref[0].astype(BF16)
    for n in range(0, D_IN, IN_PROJ_COLS):
        z_ref[:, n:n + IN_PROJ_COLS] = _dot(xb, w_in_ref[:, n:n + IN_PROJ_COLS])


def _cross_attn(z_ref, kt_ref, v_ref, cat_ref):
    c = XA_DH ** -0.5 * LOG2E
    for h in range(XA_HEADS):
        hs = slice(h * XA_DH, (h + 1) * XA_DH)
        q = z_ref[:, 4 * D_MIX + h * XA_DH:4 * D_MIX + (h + 1) * XA_DH].astype(BF16)
        s = _dot(q, kt_ref[0, hs, :])
        m = jnp.max(s, axis=-1, keepdims=True)
        e = jnp.exp2((s - m) * c)
        l = jnp.sum(e, axis=-1, keepdims=True)
        o = _dot(e.astype(BF16), v_ref[0, :, hs]) * (1.0 / l)
        cat_ref[:, D_MIX + h * XA_DH:D_MIX + (h + 1) * XA_DH] = o.astype(BF16)


def _out_proj_norm(x_ref, cat_ref, w_out_ref, g_ref, b_ref, o_ref):
    y = _dot(cat_ref[...], w_out_ref[...])
    o_ref[0] = _layer_norm_rows(DN_ALPHA * x_ref[0] + y, g_ref[...], b_ref[...])


def _hgrn_layer_kernel(x_ref, w_in_ref, w_out_ref, g_ref, b_ref, kt_ref, v_ref,
                       lbl_ref, nw_ref, o_ref, z_ref, cat_ref, st_ref, *, lb_index):
    @pl.when(pl.program_id(1) == 0)
    def _():
        st_ref[...] = jnp.zeros_like(st_ref)

    _in_proj(x_ref, w_in_ref, z_ref)
    _cross_attn(z_ref, kt_ref, v_ref, cat_ref)

    lg = lbl_ref[...]
    lg = jnp.exp(lg - jnp.max(lg, axis=0, keepdims=True))
    lb = jnp.sum(lg[:lb_index + 1], axis=0, keepdims=True) / jnp.sum(lg, axis=0, keepdims=True)
    nw = nw_ref[...]

    c_len = HG_CHUNK
    row = lax.broadcasted_iota(jnp.int32, (c_len, c_len), 0)
    col = lax.broadcasted_iota(jnp.int32, (c_len, c_len), 1)
    causal = col <= row
    tri = causal.astype(BF16)

    def chunk(ci, carry):
        r0 = pl.multiple_of(ci * c_len, c_len)
        rows = pl.ds(r0, c_len)
        zq = z_ref[rows, 0:D_MIX]
        zf = z_ref[rows, D_MIX:2 * D_MIX]
        zi = z_ref[rows, 2 * D_MIX:3 * D_MIX]
        zg = z_ref[rows, 3 * D_MIX:4 * D_MIX]
        f = lb + (1.0 - lb) * _sigmoid(zf)
        logf = jnp.log(f)
        kk = 1.0 - f
        hi = logf.astype(BF16)
        lo = (logf - hi.astype(F32)).astype(BF16)
        bc = _dot(tri, hi) + _dot(tri, lo)
        b_last = bc[c_len - 1:c_len, :]
        mid = 0.5 * b_last
        e_mid = jnp.exp(mid)
        qh = zq * _sigmoid(zq) * jnp.exp(bc - mid)
        kh = kk * jnp.exp(mid - bc)
        q_in = (qh * e_mid).astype(BF16)
        k_dec = (kh * e_mid).astype(BF16)
        qh = qh.astype(BF16)
        kh = kh.astype(BF16)
        vb = zi.astype(BF16)
        decay = jnp.exp(b_last)
        gate = zg * _sigmoid(zg)
        for h in range(HG_HEADS):
            hs = slice(h * HG_DK, (h + 1) * HG_DK)
            st = st_ref[h]
            a = jnp.where(causal, _dot_nt(qh[:, hs], kh[:, hs]), 0.0).astype(BF16)
            o = _dot_nt(q_in[:, hs], st.astype(BF16)) + _dot(a, vb[:, hs])
            st_ref[h] = st * decay[:, hs] + _dot_tn(vb[:, hs], k_dec[:, hs])
            o = o * lax.rsqrt(jnp.mean(o * o, axis=-1, keepdims=True) + RMS_EPS) * nw
            cat_ref[rows, hs] = (o * gate[:, hs]).astype(BF16)
        return carry

    lax.fori_loop(0, TOKEN_TILE // c_len, chunk, 0)
    _out_proj_norm(x_ref, cat_ref, w_out_ref, g_ref, b_ref, o_ref)


def _conv_layer_kernel(x_ref, w_in_ref, w_out_ref, g_ref, b_ref, kt_ref, v_ref,
                       cw_ref, o_ref, z_ref, cat_ref, u_ref):
    t = TOKEN_TILE

    @pl.when(pl.program_id(1) == 0)
    def _():
        u_ref[0:8, :] = jnp.zeros((8, D_MIX), F32)

    _in_proj(x_ref, w_in_ref, z_ref)
    _cross_attn(z_ref, kt_ref, v_ref, cat_ref)

    u = z_ref[:, D_MIX:2 * D_MIX] * z_ref[:, 2 * D_MIX:3 * D_MIX]
    u_ref[8:8 + t, :] = u
    cw = cw_ref[...]
    conv = (cw[0:1] * u_ref[8 - 2:8 - 2 + t, :] + cw[1:2] * u_ref[8 - 1:8 - 1 + t, :]
            + cw[2:3] * u)
    u_ref[0:8, :] = u_ref[t:t + 8, :]
    zg = z_ref[:, 3 * D_MIX:4 * D_MIX]
    mix = z_ref[:, 0:D_MIX] * conv * (zg * _sigmoid(zg))
    cat_ref[:, 0:D_MIX] = mix.astype(BF16)
    _out_proj_norm(x_ref, cat_ref, w_out_ref, g_ref, b_ref, o_ref)


def _layer_call(body, name, x, w_in, w_out, g, b, kt, v, extra, extra_scratch):
    bsz, slen, _ = x.shape
    t = TOKEN_TILE
    const2 = lambda i, j: (0, 0)
    single = pl.Buffered(1)
    in_specs = [
        pl.BlockSpec((1, t, D_MODEL), lambda i, j: (i, j, 0)),
        pl.BlockSpec((D_MODEL, D_IN), const2, pipeline_mode=single),
        pl.BlockSpec((D_CAT, D_MODEL), const2, pipeline_mode=single),
        pl.BlockSpec((1, D_MODEL), const2),
        pl.BlockSpec((1, D_MODEL), const2),
        pl.BlockSpec((1, D_XA, N_MEM), lambda i, j: (i, 0, 0)),
        pl.BlockSpec((1, N_MEM, D_XA), lambda i, j: (i, 0, 0)),
    ] + [pl.BlockSpec(e.shape, const2) for e in extra]
    return pl.pallas_call(
        body,
        grid=(bsz, slen // t),
        in_specs=in_specs,
        out_specs=pl.BlockSpec((1, t, D_MODEL), lambda i, j: (i, j, 0)),
        out_shape=jax.ShapeDtypeStruct((bsz, slen, D_MODEL), F32),
        scratch_shapes=[
            pltpu.VMEM((t, D_IN), F32),
            pltpu.VMEM((t, D_CAT), BF16),
        ] + extra_scratch,
        compiler_params=pltpu.CompilerParams(
            dimension_semantics=("arbitrary", "arbitrary"),
            vmem_limit_bytes=VMEM_LIMIT_BYTES),
        name=name,
    )(x, w_in, w_out, g, b, kt, v, *extra)


def kernel(x, mem, w_in, w_out, ln_g, ln_b, hgrn_lb_logits, hgrn_norm_w, conv_w,
           mem_ln_g, mem_ln_b, w_mem_kv):
    bsz, slen, d_model = x.shape
    assert d_model == D_MODEL and slen % TOKEN_TILE == 0
    assert w_in.shape == (DEPTH, D_MODEL, D_IN) and w_out.shape == (DEPTH, D_CAT, D_MODEL)
    assert mem.shape == (bsz, N_MEM, D_MODEL)

    kt, v = _mem_kv(mem, w_mem_kv.astype(BF16), mem_ln_g.reshape(1, -1), mem_ln_b.reshape(1, -1))
    w_in_b = w_in.astype(BF16)
    w_out_b = w_out.astype(BF16)

    for layer in range(DEPTH):
        j = layer // 2
        g = ln_g[layer].reshape(1, -1)
        b = ln_b[layer].reshape(1, -1)
        if layer % 2 == 0:
            body = functools.partial(_hgrn_layer_kernel, lb_index=j)
            extra = [hgrn_lb_logits, hgrn_norm_w[j].reshape(1, -1)]
            scratch = [pltpu.VMEM((HG_HEADS, HG_DK, HG_DK), F32)]
            name = "hgrn_layer"
        else:
            body = _conv_layer_kernel
            extra = [conv_w[j]]
            scratch = [pltpu.VMEM((TOKEN_TILE + 8, D_MIX), F32)]
            name = "conv_layer"
        x = _layer_call(body, name, x, w_in_b[layer], w_out_b[layer], g, b, kt, v, extra, scratch)
    return x
```

```python
import functools
import math

import jax
import jax.numpy as jnp
from jax import lax
from jax.experimental import pallas as pl
from jax.experimental.pallas import tpu as pltpu

D_MODEL = 1024
N_MEM = 256
HG_HEADS = 8
HG_DK = 128
D_MIX = 1024
CONV_WIDTH = 3
XA_HEADS = 4
XA_DH = 128
D_XA = XA_HEADS * XA_DH
D_IN = 4 * D_MIX + D_XA
D_CAT = D_MIX + D_XA
DEPTH = 2
DN_ALPHA = (2 * DEPTH) ** 0.25
LN_EPS = 1e-5
RMS_EPS = 1e-5

TOKEN_TILE = 512
HG_CHUNK = 128
VMEM_LIMIT_BYTES = 56 * 1024 * 1024

F32 = jnp.float32
BF16 = jnp.bfloat16
LOG2E = math.log2(math.e)


def _layer_norm_rows(v, g, b):
    mu = jnp.mean(v, axis=-1, keepdims=True)
    d = v - mu
    var = jnp.mean(d * d, axis=-1, keepdims=True)
    return d * lax.rsqrt(var + LN_EPS) * g + b


def _sigmoid(v):
    return 1.0 / (1.0 + jnp.exp(-v))


def _dot(a, b):
    return jnp.dot(a, b, preferred_element_type=F32)


def _dot_nt(a, b):
    return lax.dot_general(a, b, (((1,), (1,)), ((), ())), preferred_element_type=F32)


def _mem_kv_kernel(mem_ref, w_ref, g_ref, b_ref, kt_ref, v_ref):
    m = _layer_norm_rows(mem_ref[0], g_ref[...], b_ref[...])
    kv = _dot(m.astype(BF16), w_ref[...])
    kt_ref[0] = kv[:, :D_XA].T.astype(BF16)
    v_ref[0] = kv[:, D_XA:].astype(BF16)


def _mem_kv(mem, w_mem_kv, g, b):
    bsz = mem.shape[0]
    return pl.pallas_call(
        _mem_kv_kernel,
        grid=(bsz,),
        in_specs=[
            pl.BlockSpec((1, N_MEM, D_MODEL), lambda i: (i, 0, 0)),
            pl.BlockSpec((D_MODEL, 2 * D_XA), lambda i: (0, 0)),
            pl.BlockSpec((1, D_MODEL), lambda i: (0, 0)),
            pl.BlockSpec((1, D_MODEL), lambda i: (0, 0)),
        ],
        out_specs=[
            pl.BlockSpec((1, D_XA, N_MEM), lambda i: (i, 0, 0)),
            pl.BlockSpec((1, N_MEM, D_XA), lambda i: (i, 0, 0)),
        ],
        out_shape=[
            jax.ShapeDtypeStruct((bsz, D_XA, N_MEM), BF16),
            jax.ShapeDtypeStruct((bsz, N_MEM, D_XA), BF16),
        ],
        compiler_params=pltpu.CompilerParams(
            dimension_semantics=("arbitrary",), vmem_limit_bytes=VMEM_LIMIT_BYTES),
        name="mem_kv",
    )(mem, w_mem_kv, g, b)


GROUP = HG_DK
N_GROUPS = D_MIX // GROUP
GROUP_COLS = 4 * GROUP


def _group_proj(xb, w_in_ref, g):
    return _dot(xb, w_in_ref[:, g * GROUP_COLS:(g + 1) * GROUP_COLS])


def _cross_attn(xb, w_in_ref, kt_ref, v_ref, cat_ref):
    c = XA_DH ** -0.5 * LOG2E
    zx = _dot(xb, w_in_ref[:, 4 * D_MIX:])
    for h in range(XA_HEADS):
        hs = slice(h * XA_DH, (h + 1) * XA_DH)
        s = _dot(zx[:, hs].astype(BF16), kt_ref[0, hs, :])
        m = jnp.max(s, axis=-1, keepdims=True)
        e = jnp.exp2((s - m) * c)
        l = jnp.sum(e, axis=-1, keepdims=True)
        o = _dot(e.astype(BF16), v_ref[0, :, hs]) * (1.0 / l)
        cat_ref[:, D_MIX + h * XA_DH:D_MIX + (h + 1) * XA_DH] = o.astype(BF16)


def _out_proj_norm(x_ref, cat_ref, w_out_ref, g_ref, b_ref, o_ref):
    y = _dot(cat_ref[...], w_out_ref[...])
    o_ref[0] = _layer_norm_rows(DN_ALPHA * x_ref[0] + y, g_ref[...], b_ref[...])


def _hgrn_layer_kernel(x_ref, w_in_ref, w_out_ref, g_ref, b_ref, kt_ref, v_ref,
                       lbl_ref, nw_ref, o_ref, cat_ref, st_ref, *z_refs, lb_index):
    @pl.when(pl.program_id(1) == 0)
    def _():
        st_ref[...] = jnp.zeros_like(st_ref)

    xb = x_ref[0].astype(BF16)

    lg = lbl_ref[...]
    lg = jnp.exp(lg - jnp.max(lg, axis=0, keepdims=True))
    lb_all = jnp.sum(lg[:lb_index + 1], axis=0, keepdims=True) / jnp.sum(lg, axis=0, keepdims=True)
    nw = nw_ref[...]

    c_len = HG_CHUNK
    n_chunks = TOKEN_TILE // c_len
    n_items = HG_HEADS * n_chunks
    row = lax.broadcasted_iota(jnp.int32, (c_len, c_len), 0)
    col = lax.broadcasted_iota(jnp.int32, (c_len, c_len), 1)
    causal = col <= row
    tri = causal.astype(BF16)
    half = GROUP_COLS // 2

    def proj_piece(q):
        h, n = divmod(q, 2)
        cols = slice(h * GROUP_COLS + n * half, h * GROUP_COLS + (n + 1) * half)
        z_refs[h][:, n * half:(n + 1) * half] = _dot(xb, w_in_ref[:, cols])

    def stage_a(i):
        h, ci = divmod(i, n_chunks)
        rows = slice(ci * c_len, (ci + 1) * c_len)
        lb = lb_all[:, h * HG_DK:(h + 1) * HG_DK]
        zf = z_refs[h][rows, HG_DK:2 * HG_DK]
        f = lb + (1.0 - lb) * _sigmoid(zf)
        logf = jnp.log(f)
        hi = logf.astype(BF16)
        lo = (logf - hi.astype(F32)).astype(BF16)
        bb = _dot(tri, jnp.concatenate([hi, lo], axis=1))
        return dict(kk=1.0 - f, bb=bb)

    def stage_b(i, v):
        h, ci = divmod(i, n_chunks)
        rows = slice(ci * c_len, (ci + 1) * c_len)
        zq = z_refs[h][rows, 0:HG_DK]
        zi = z_refs[h][rows, 2 * HG_DK:3 * HG_DK]
        bc = v["bb"][:, :HG_DK] + v["bb"][:, HG_DK:]
        b_last = bc[c_len - 1:c_len, :]
        mid = 0.5 * b_last
        e_mid = jnp.exp(mid)
        qh = zq * _sigmoid(zq) * jnp.exp(bc - mid)
        kh = v["kk"] * jnp.exp(mid - bc)
        q_in = (qh * e_mid).astype(BF16)
        k_dec = (kh * e_mid).astype(BF16)
        vt = zi.T.astype(BF16)
        return dict(q_in=q_in, vt=vt, decay=jnp.exp(b_last),
                    scores=_dot_nt(qh.astype(BF16), kh.astype(BF16)), kv=_dot(vt, k_dec))

    def stage_c(i, v, st):
        a = jnp.where(causal, v["scores"], 0.0).astype(BF16)
        o = _dot_nt(jnp.concatenate([v["q_in"], a], axis=1),
                    jnp.concatenate([st.astype(BF16), v["vt"]], axis=1))
        return o, st * v["decay"] + v["kv"]

    def stage_d(i, o):
        h, ci = divmod(i, n_chunks)
        rows = slice(ci * c_len, (ci + 1) * c_len)
        zg = z_refs[h][rows, 3 * HG_DK:4 * HG_DK]
        o = o * lax.rsqrt(jnp.mean(o * o, axis=-1, keepdims=True) + RMS_EPS) * nw
        cat_ref[rows, h * HG_DK:(h + 1) * HG_DK] = (o * (zg * _sigmoid(zg))).astype(BF16)

    proj_lead = 6
    n_pieces = 2 * HG_HEADS
    emitted = 0
    while 2 * emitted - proj_lead < -3:
        proj_piece(emitted)
        emitted += 1
    va, vb, vo, st = {}, {}, {}, None
    for s in range(-3, n_items):
        if emitted < n_pieces and 2 * emitted - proj_lead <= s:
            proj_piece(emitted)
            emitted += 1
        if 0 <= s + 3 < n_items:
            va[s + 3] = stage_a(s + 3)
        if 0 <= s + 2 < n_items:
            vb[s + 2] = stage_b(s + 2, va.pop(s + 2))
        if 0 <= s + 1 < n_items:
            h, ci = divmod(s + 1, n_chunks)
            if ci == 0:
                st = st_ref[h]
            vo[s + 1], st = stage_c(s + 1, vb.pop(s + 1), st)
            if ci == n_chunks - 1:
                st_ref[h] = st
        if 0 <= s:
            stage_d(s, vo.pop(s))

    _cross_attn(xb, w_in_ref, kt_ref, v_ref, cat_ref)
    _out_proj_norm(x_ref, cat_ref, w_out_ref, g_ref, b_ref, o_ref)


def _conv_layer_kernel(x_ref, w_in_ref, w_out_ref, g_ref, b_ref, kt_ref, v_ref,
                       cw_ref, o_ref, cat_ref, u_ref):
    t = TOKEN_TILE

    @pl.when(pl.program_id(1) == 0)
    def _():
        u_ref[0:8, :] = jnp.zeros((8, D_MIX), F32)

    xb = x_ref[0].astype(BF16)
    cw = cw_ref[...]
    for g in range(N_GROUPS):
        gs = slice(g * GROUP, (g + 1) * GROUP)
        z = _group_proj(xb, w_in_ref, g)
        u = z[:, GROUP:2 * GROUP] * z[:, 2 * GROUP:3 * GROUP]
        u_ref[8:8 + t, gs] = u
        conv = (cw[0:1, gs] * u_ref[8 - 2:8 - 2 + t, gs] + cw[1:2, gs] * u_ref[8 - 1:8 - 1 + t, gs]
                + cw[2:3, gs] * u)
        u_ref[0:8, gs] = u_ref[t:t + 8, gs]
        zg = z[:, 3 * GROUP:4 * GROUP]
        cat_ref[:, gs] = (z[:, 0:GROUP] * conv * (zg * _sigmoid(zg))).astype(BF16)

    _cross_attn(xb, w_in_ref, kt_ref, v_ref, cat_ref)
    _out_proj_norm(x_ref, cat_ref, w_out_ref, g_ref, b_ref, o_ref)


def _layer_call(body, name, x, w_in, w_out, g, b, kt, v, extra, extra_scratch):
    bsz, slen, _ = x.shape
    t = TOKEN_TILE
    const2 = lambda i, j: (0, 0)
    single = pl.Buffered(1)
    in_specs = [
        pl.BlockSpec((1, t, D_MODEL), lambda i, j: (i, j, 0)),
        pl.BlockSpec((D_MODEL, D_IN), const2, pipeline_mode=single),
        pl.BlockSpec((D_CAT, D_MODEL), const2, pipeline_mode=single),
        pl.BlockSpec((1, D_MODEL), const2),
        pl.BlockSpec((1, D_MODEL), const2),
        pl.BlockSpec((1, D_XA, N_MEM), lambda i, j: (i, 0, 0)),
        pl.BlockSpec((1, N_MEM, D_XA), lambda i, j: (i, 0, 0)),
    ] + [pl.BlockSpec(e.shape, const2) for e in extra]
    return pl.pallas_call(
        body,
        grid=(bsz, slen // t),
        in_specs=in_specs,
        out_specs=pl.BlockSpec((1, t, D_MODEL), lambda i, j: (i, j, 0)),
        out_shape=jax.ShapeDtypeStruct((bsz, slen, D_MODEL), F32),
        scratch_shapes=[pltpu.VMEM((t, D_CAT), BF16)] + extra_scratch,
        compiler_params=pltpu.CompilerParams(
            dimension_semantics=("arbitrary", "arbitrary"),
            vmem_limit_bytes=VMEM_LIMIT_BYTES),
        name=name,
    )(x, w_in, w_out, g, b, kt, v, *extra)


def kernel(x, mem, w_in, w_out, ln_g, ln_b, hgrn_lb_logits, hgrn_norm_w, conv_w,
           mem_ln_g, mem_ln_b, w_mem_kv):
    bsz, slen, d_model = x.shape
    assert d_model == D_MODEL and slen % TOKEN_TILE == 0
    assert w_in.shape == (DEPTH, D_MODEL, D_IN) and w_out.shape == (DEPTH, D_CAT, D_MODEL)
    assert mem.shape == (bsz, N_MEM, D_MODEL)

    kt, v = _mem_kv(mem, w_mem_kv.astype(BF16), mem_ln_g.reshape(1, -1), mem_ln_b.reshape(1, -1))
    w_mix = w_in[:, :, :4 * D_MIX].reshape(DEPTH, D_MODEL, 4, N_GROUPS, GROUP)
    w_mix = w_mix.transpose(0, 1, 3, 2, 4).reshape(DEPTH, D_MODEL, 4 * D_MIX)
    w_in_b = jnp.concatenate([w_mix, w_in[:, :, 4 * D_MIX:]], axis=2).astype(BF16)
    w_out_b = w_out.astype(BF16)

    for layer in range(DEPTH):
        j = layer // 2
        g = ln_g[layer].reshape(1, -1)
        b = ln_b[layer].reshape(1, -1)
        if layer % 2 == 0:
            body = functools.partial(_hgrn_layer_kernel, lb_index=j)
            extra = [hgrn_lb_logits, hgrn_norm_w[j].reshape(1, -1)]
            scratch = [pltpu.VMEM((HG_HEADS, HG_DK, HG_DK), F32)] + [
                pltpu.VMEM((TOKEN_TILE, GROUP_COLS), F32) for _ in range(HG_HEADS)]
            name = "hgrn_layer"
        else:
            body = _conv_layer_kernel
            extra = [conv_w[j]]
            scratch = [pltpu.VMEM((TOKEN_TILE + 8, D_MIX), F32)]
            name = "conv_layer"
        x = _layer_call(body, name, x, w_in_b[layer], w_out_b[layer], g, b, kt, v, extra, scratch)
    return x
```

```python
import functools
import math

import jax
import jax.numpy as jnp
from jax import lax
from jax.experimental import pallas as pl
from jax.experimental.pallas import tpu as pltpu

D_MODEL = 1024
N_MEM = 256
HG_HEADS = 8
HG_DK = 128
D_MIX = 1024
CONV_WIDTH = 3
XA_HEADS = 4
XA_DH = 128
D_XA = XA_HEADS * XA_DH
D_IN = 4 * D_MIX + D_XA
D_CAT = D_MIX + D_XA
DEPTH = 2
DN_ALPHA = (2 * DEPTH) ** 0.25
LN_EPS = 1e-5
RMS_EPS = 1e-5

TOKEN_TILE = 512
HG_CHUNK = 128
GROUP = HG_DK
N_GROUPS = D_MIX // GROUP
OUT_ROW_SPLIT = 2
SAFE_BLOCK = 8
MAX_EXPONENT = -1.0
VMEM_LIMIT_BYTES = 56 * 1024 * 1024

F32 = jnp.float32
BF16 = jnp.bfloat16
LOG2E = math.log2(math.e)


def _layer_norm_rows(v, g, b):
    mu = jnp.mean(v, axis=-1, keepdims=True)
    d = v - mu
    var = jnp.mean(d * d, axis=-1, keepdims=True)
    return d * lax.rsqrt(var + LN_EPS) * g + b


def _sigmoid(v):
    return 1.0 / (1.0 + jnp.exp(-v))


def _dot(a, b):
    return jnp.dot(a, b, preferred_element_type=F32)


def _dot_nt(a, b):
    return lax.dot_general(a, b, (((1,), (1,)), ((), ())), preferred_element_type=F32)


def _dot_tn(a, b):
    return lax.dot_general(a, b, (((0,), (0,)), ((), ())), preferred_element_type=F32)


def _mem_kv_kernel(mem_ref, w_ref, g_ref, b_ref, kt_ref, v_ref):
    m = _layer_norm_rows(mem_ref[0], g_ref[...], b_ref[...])
    kv = _dot(m.astype(BF16), w_ref[...])
    kt_ref[0] = kv[:, :D_XA].T.astype(BF16)
    v_ref[0] = kv[:, D_XA:].astype(BF16)


def _mem_kv(mem, w_mem_kv, g, b):
    bsz = mem.shape[0]
    return pl.pallas_call(
        _mem_kv_kernel,
        grid=(bsz,),
        in_specs=[
            pl.BlockSpec((1, N_MEM, D_MODEL), lambda i: (i, 0, 0)),
            pl.BlockSpec((D_MODEL, 2 * D_XA), lambda i: (0, 0)),
            pl.BlockSpec((1, D_MODEL), lambda i: (0, 0)),
            pl.BlockSpec((1, D_MODEL), lambda i: (0, 0)),
        ],
        out_specs=[
            pl.BlockSpec((1, D_XA, N_MEM), lambda i: (i, 0, 0)),
            pl.BlockSpec((1, N_MEM, D_XA), lambda i: (i, 0, 0)),
        ],
        out_shape=[
            jax.ShapeDtypeStruct((bsz, D_XA, N_MEM), BF16),
            jax.ShapeDtypeStruct((bsz, N_MEM, D_XA), BF16),
        ],
        compiler_params=pltpu.CompilerParams(
            dimension_semantics=("arbitrary",), vmem_limit_bytes=VMEM_LIMIT_BYTES),
        name="mem_kv",
    )(mem, w_mem_kv, g, b)


def _mix_proj(xb, w_in_ref, g, parts):
    w = [w_in_ref[:, p * D_MIX + g * GROUP:p * D_MIX + (g + 1) * GROUP] for p in parts]
    return _dot(xb, jnp.concatenate(w, axis=1))


def _forget_lower_bound(lbl_ref, lb_index):
    lg = lbl_ref[...]
    lg = jnp.exp(lg - jnp.max(lg, axis=0, keepdims=True))
    return jnp.sum(lg[:lb_index + 1], axis=0, keepdims=True) / jnp.sum(lg, axis=0, keepdims=True)


def _cross_attn_tasks(xb, w_in_ref, kt_ref, v_ref, cat_ref, zx_ref):
    c = XA_DH ** -0.5 * LOG2E
    scores = {}

    def queries(n):
        cols = slice(n * 2 * XA_DH, (n + 1) * 2 * XA_DH)
        zx_ref[:, cols] = _dot(xb, w_in_ref[:, 4 * D_MIX + cols.start:4 * D_MIX + cols.stop]).astype(BF16)

    def logits(h):
        hs = slice(h * XA_DH, (h + 1) * XA_DH)
        scores[h] = _dot(zx_ref[:, hs], kt_ref[0, hs, :])

    def read(h):
        hs = slice(h * XA_DH, (h + 1) * XA_DH)
        s = scores.pop(h)
        e = jnp.exp2((s - jnp.max(s, axis=-1, keepdims=True)) * c)
        l = jnp.sum(e, axis=-1, keepdims=True)
        o = _dot(e.astype(BF16), v_ref[0, :, hs]) * (1.0 / l)
        cat_ref[:, D_MIX + h * XA_DH:D_MIX + (h + 1) * XA_DH] = o.astype(BF16)

    tasks = [functools.partial(queries, 0), functools.partial(queries, 1)]
    for h in range(XA_HEADS):
        tasks += [functools.partial(logits, h), functools.partial(read, h)]
    return tasks


def _out_proj_norm(x_ref, cat_ref, w_out_ref, g_ref, b_ref, o_ref):
    t = TOKEN_TILE // OUT_ROW_SPLIT
    ys = [_dot(cat_ref[r * t:(r + 1) * t, :], w_out_ref[...]) for r in range(OUT_ROW_SPLIT)]
    for r, y in enumerate(ys):
        rows = slice(r * t, (r + 1) * t)
        o_ref[0, rows, :] = _layer_norm_rows(DN_ALPHA * x_ref[0, rows, :] + y, g_ref[...], b_ref[...])


def _hgrn_layer_kernel(x_ref, w_in_ref, w_out_ref, g_ref, b_ref, kt_ref, v_ref,
                       lbl_ref, nw_ref, o_ref, cat_ref, zx_ref, st_ref, *z_refs, lb_index):
    @pl.when(pl.program_id(1) == 0)
    def _():
        st_ref[...] = jnp.zeros_like(st_ref)

    xb = x_ref[0].astype(BF16)
    lb_all = _forget_lower_bound(lbl_ref, lb_index)
    nw = nw_ref[...]

    c_len = HG_CHUNK
    n_chunks = TOKEN_TILE // c_len
    n_items = HG_HEADS * n_chunks
    row = lax.broadcasted_iota(jnp.int32, (c_len, c_len), 0)
    col = lax.broadcasted_iota(jnp.int32, (c_len, c_len), 1)
    causal = col <= row
    tri = causal.astype(BF16)
    half = 2 * GROUP

    def proj_piece(q):
        h, n = divmod(q, 2)
        z_refs[h][:, n * half:(n + 1) * half] = _mix_proj(xb, w_in_ref, h, (2 * n, 2 * n + 1))

    def stage_a(i):
        h, ci = divmod(i, n_chunks)
        rows = slice(ci * c_len, (ci + 1) * c_len)
        lb = lb_all[:, h * HG_DK:(h + 1) * HG_DK]
        zf = z_refs[h][rows, HG_DK:2 * HG_DK]
        f = lb + (1.0 - lb) * _sigmoid(zf)
        logf = jnp.log(f)
        hi = logf.astype(BF16)
        lo = (logf - hi.astype(F32)).astype(BF16)
        bb = _dot(tri, jnp.concatenate([hi, lo], axis=1))
        return dict(kk=1.0 - f, bb=bb)

    def stage_b(i, v):
        h, ci = divmod(i, n_chunks)
        rows = slice(ci * c_len, (ci + 1) * c_len)
        zq = z_refs[h][rows, 0:HG_DK]
        zi = z_refs[h][rows, 2 * HG_DK:3 * HG_DK]
        bc = v["bb"][:, :HG_DK] + v["bb"][:, HG_DK:]
        b_last = bc[c_len - 1:c_len, :]
        mid = 0.5 * b_last
        e_mid = jnp.exp(mid)
        qh = zq * _sigmoid(zq) * jnp.exp(bc - mid)
        kh = v["kk"] * jnp.exp(mid - bc)
        q_in = (qh * e_mid).astype(BF16)
        k_dec = (kh * e_mid).astype(BF16)
        vt = zi.T.astype(BF16)
        return dict(q_in=q_in, vt=vt, decay=jnp.exp(b_last),
                    scores=_dot_nt(qh.astype(BF16), kh.astype(BF16)), kv=_dot(vt, k_dec))

    def stage_c(v, st):
        a = jnp.where(causal, v["scores"], 0.0).astype(BF16)
        o = _dot_nt(jnp.concatenate([v["q_in"], a], axis=1),
                    jnp.concatenate([st.astype(BF16), v["vt"]], axis=1))
        return o, st * v["decay"] + v["kv"]

    def stage_d(i, o):
        h, ci = divmod(i, n_chunks)
        rows = slice(ci * c_len, (ci + 1) * c_len)
        zg = z_refs[h][rows, 3 * HG_DK:4 * HG_DK]
        o = o * lax.rsqrt(jnp.mean(o * o, axis=-1, keepdims=True) + RMS_EPS) * nw
        cat_ref[rows, h * HG_DK:(h + 1) * HG_DK] = (o * (zg * _sigmoid(zg))).astype(BF16)

    proj_lead = 6
    n_pieces = 2 * HG_HEADS
    attn_tasks = _cross_attn_tasks(xb, w_in_ref, kt_ref, v_ref, cat_ref, zx_ref)
    emitted = 0
    while 2 * emitted - proj_lead < -3:
        proj_piece(emitted)
        emitted += 1
    va, vb, vo, st = {}, {}, {}, None
    for s in range(-3, n_items):
        if emitted < n_pieces and 2 * emitted - proj_lead <= s:
            proj_piece(emitted)
            emitted += 1
        elif attn_tasks and s >= 0:
            attn_tasks.pop(0)()
        if 0 <= s + 3 < n_items:
            va[s + 3] = stage_a(s + 3)
        if 0 <= s + 2 < n_items:
            vb[s + 2] = stage_b(s + 2, va.pop(s + 2))
        if 0 <= s + 1 < n_items:
            h, ci = divmod(s + 1, n_chunks)
            if ci == 0:
                st = st_ref[h]
            vo[s + 1], st = stage_c(vb.pop(s + 1), st)
            if ci == n_chunks - 1:
                st_ref[h] = st
        if 0 <= s:
            stage_d(s, vo.pop(s))
    while attn_tasks:
        attn_tasks.pop(0)()

    _out_proj_norm(x_ref, cat_ref, w_out_ref, g_ref, b_ref, o_ref)


def _hgrn_safe_layer_kernel(x_ref, w_in_ref, w_out_ref, g_ref, b_ref, kt_ref, v_ref,
                            lbl_ref, nw_ref, o_ref, cat_ref, zx_ref, st_ref, z_ref, *, lb_index):
    @pl.when(pl.program_id(1) == 0)
    def _():
        st_ref[...] = jnp.zeros_like(st_ref)

    xb = x_ref[0].astype(BF16)
    for p in range(4):
        for g in range(0, N_GROUPS, 2):
            cols = slice(p * D_MIX + g * GROUP, p * D_MIX + (g + 2) * GROUP)
            z_ref[:, cols] = _dot(xb, w_in_ref[:, cols])
    for task in _cross_attn_tasks(xb, w_in_ref, kt_ref, v_ref, cat_ref, zx_ref):
        task()

    lb_all = _forget_lower_bound(lbl_ref, lb_index)
    nw = nw_ref[...]
    n = SAFE_BLOCK
    row = lax.broadcasted_iota(jnp.int32, (n, HG_DK), 0)
    shifts = [1 << j for j in range(n.bit_length() - 1)]

    def block(rows, h):
        hs = slice(h * HG_DK, (h + 1) * HG_DK)
        zq = z_ref[rows, h * HG_DK:(h + 1) * HG_DK]
        zf = z_ref[rows, D_MIX + h * HG_DK:D_MIX + (h + 1) * HG_DK]
        v = z_ref[rows, 2 * D_MIX + h * HG_DK:2 * D_MIX + (h + 1) * HG_DK]
        zg = z_ref[rows, 3 * D_MIX + h * HG_DK:3 * D_MIX + (h + 1) * HG_DK]
        lb = lb_all[:, hs]
        q = zq * _sigmoid(zq)
        f = lb + (1.0 - lb) * _sigmoid(zf)
        k = 1.0 - f
        pre = f
        suf = jnp.where(row < n - 1, pltpu.roll(f, n - 1, 0), 1.0)
        for sh in shifts:
            pre = pre * jnp.where(row >= sh, pltpu.roll(pre, sh, 0), 1.0)
            suf = suf * jnp.where(row + sh < n, pltpu.roll(suf, n - sh, 0), 1.0)
        st = st_ref[h]
        o = _dot_nt(q * pre, st)
        dec = None
        for lag in range(n):
            dec = jnp.ones_like(f) if lag == 0 else dec * pltpu.roll(f, lag - 1, 0)
            k_s = k if lag == 0 else pltpu.roll(k, lag, 0)
            v_s = v if lag == 0 else pltpu.roll(v, lag, 0)
            a = jnp.sum(q * k_s * dec, axis=-1, keepdims=True)
            o = o + jnp.where(row >= lag, a * v_s, 0.0)
        st_ref[h] = st * pre[n - 1:n, :] + _dot_tn(v, k * suf)
        o = o * lax.rsqrt(jnp.mean(o * o, axis=-1, keepdims=True) + RMS_EPS) * nw
        return o * (zg * _sigmoid(zg))

    def block_pair(bi, carry):
        r0 = pl.multiple_of(bi * 2 * n, 2 * n)
        for h in range(HG_HEADS):
            o = [block(pl.ds(r0 + j * n, n), h) for j in range(2)]
            cat_ref[pl.ds(r0, 2 * n), h * HG_DK:(h + 1) * HG_DK] = jnp.concatenate(o, axis=0).astype(BF16)
        return carry

    lax.fori_loop(0, TOKEN_TILE // (2 * n), block_pair, 0)
    _out_proj_norm(x_ref, cat_ref, w_out_ref, g_ref, b_ref, o_ref)


def _conv_layer_kernel(x_ref, w_in_ref, w_out_ref, g_ref, b_ref, kt_ref, v_ref,
                       cw_ref, o_ref, cat_ref, zx_ref, u_ref):
    t = TOKEN_TILE

    @pl.when(pl.program_id(1) == 0)
    def _():
        u_ref[0:8, :] = jnp.zeros((8, D_MIX), F32)

    xb = x_ref[0].astype(BF16)
    cw = cw_ref[...]

    def mix(g, z):
        gs = slice(g * GROUP, (g + 1) * GROUP)
        u = z[:, GROUP:2 * GROUP] * z[:, 2 * GROUP:3 * GROUP]
        u_ref[8:8 + t, gs] = u
        conv = (cw[0:1, gs] * u_ref[8 - 2:8 - 2 + t, gs] + cw[1:2, gs] * u_ref[8 - 1:8 - 1 + t, gs]
                + cw[2:3, gs] * u)
        u_ref[0:8, gs] = u_ref[t:t + 8, gs]
        zg = z[:, 3 * GROUP:4 * GROUP]
        cat_ref[:, gs] = (z[:, 0:GROUP] * conv * (zg * _sigmoid(zg))).astype(BF16)

    attn_tasks = _cross_attn_tasks(xb, w_in_ref, kt_ref, v_ref, cat_ref, zx_ref)
    z_next = _mix_proj(xb, w_in_ref, 0, (0, 1, 2, 3))
    for g in range(N_GROUPS):
        z = z_next
        if g + 1 < N_GROUPS:
            z_next = _mix_proj(xb, w_in_ref, g + 1, (0, 1, 2, 3))
        if attn_tasks:
            attn_tasks.pop(0)()
        mix(g, z)
    while attn_tasks:
        attn_tasks.pop(0)()

    _out_proj_norm(x_ref, cat_ref, w_out_ref, g_ref, b_ref, o_ref)


def _layer_call(body, name, x, w_in, w_out, g, b, kt, v, *, extra, extra_scratch):
    bsz, slen, _ = x.shape
    t = TOKEN_TILE
    const2 = lambda i, j: (0, 0)
    single = pl.Buffered(1)
    in_specs = [
        pl.BlockSpec((1, t, D_MODEL), lambda i, j: (i, j, 0)),
        pl.BlockSpec((D_MODEL, D_IN), const2, pipeline_mode=single),
        pl.BlockSpec((D_CAT, D_MODEL), const2, pipeline_mode=single),
        pl.BlockSpec((1, D_MODEL), const2),
        pl.BlockSpec((1, D_MODEL), const2),
        pl.BlockSpec((1, D_XA, N_MEM), lambda i, j: (i, 0, 0)),
        pl.BlockSpec((1, N_MEM, D_XA), lambda i, j: (i, 0, 0)),
    ] + [pl.BlockSpec(e.shape, const2) for e in extra]
    return pl.pallas_call(
        body,
        grid=(bsz, slen // t),
        in_specs=in_specs,
        out_specs=pl.BlockSpec((1, t, D_MODEL), lambda i, j: (i, j, 0)),
        out_shape=jax.ShapeDtypeStruct((bsz, slen, D_MODEL), F32),
        scratch_shapes=[pltpu.VMEM((t, D_CAT), BF16), pltpu.VMEM((t, D_XA), BF16)] + extra_scratch,
        compiler_params=pltpu.CompilerParams(
            dimension_semantics=("arbitrary", "arbitrary"),
            vmem_limit_bytes=VMEM_LIMIT_BYTES),
        name=name,
    )(x, w_in, w_out, g, b, kt, v, *extra)


def kernel(x, mem, w_in, w_out, ln_g, ln_b, hgrn_lb_logits, hgrn_norm_w, conv_w,
           mem_ln_g, mem_ln_b, w_mem_kv):
    bsz, slen, d_model = x.shape
    assert d_model == D_MODEL and slen % TOKEN_TILE == 0
    assert w_in.shape == (DEPTH, D_MODEL, D_IN) and w_out.shape == (DEPTH, D_CAT, D_MODEL)
    assert mem.shape == (bsz, N_MEM, D_MODEL)

    kt, v = _mem_kv(mem, w_mem_kv.astype(BF16), mem_ln_g.reshape(1, -1), mem_ln_b.reshape(1, -1))
    w_in_b = w_in.astype(BF16)
    w_out_b = w_out.astype(BF16)

    for layer in range(DEPTH):
        j = layer // 2
        g = ln_g[layer].reshape(1, -1)
        b = ln_b[layer].reshape(1, -1)
        args = (x, w_in_b[layer], w_out_b[layer], g, b, kt, v)
        if layer % 2 == 0:
            extra = [hgrn_lb_logits, hgrn_norm_w[j].reshape(1, -1)]
            state = pltpu.VMEM((HG_HEADS, HG_DK, HG_DK), F32)
            fast = functools.partial(
                _layer_call, functools.partial(_hgrn_layer_kernel, lb_index=j), "hgrn_layer",
                extra=extra, extra_scratch=[state] + [
                    pltpu.VMEM((TOKEN_TILE, 4 * GROUP), F32) for _ in range(HG_HEADS)])
            safe = functools.partial(
                _layer_call, functools.partial(_hgrn_safe_layer_kernel, lb_index=j), "hgrn_safe_layer",
                extra=extra, extra_scratch=[state, pltpu.VMEM((TOKEN_TILE, 4 * D_MIX), F32)])
            log_lb = jax.nn.log_softmax(hgrn_lb_logits.astype(F32), axis=0)
            log_lb = jax.nn.logsumexp(log_lb[:j + 1], axis=0)
            fast_ok = 0.5 * HG_CHUNK * jnp.max(-log_lb) <= MAX_EXPONENT
            x = lax.cond(fast_ok, fast, safe, *args)
        else:
            x = _layer_call(_conv_layer_kernel, "conv_layer", *args, extra=[conv_w[j]],
                            extra_scratch=[pltpu.VMEM((TOKEN_TILE + 8, D_MIX), F32)])
    return x
```

```python
import functools
import math

import jax
import jax.numpy as jnp
from jax import lax
from jax.experimental import pallas as pl
from jax.experimental.pallas import tpu as pltpu

D_MODEL = 1024
N_MEM = 256
HG_HEADS = 8
HG_DK = 128
D_MIX = 1024
CONV_WIDTH = 3
XA_HEADS = 4
XA_DH = 128
D_XA = XA_HEADS * XA_DH
D_IN = 4 * D_MIX + D_XA
D_CAT = D_MIX + D_XA
DEPTH = 2
DN_ALPHA = (2 * DEPTH) ** 0.25
LN_EPS = 1e-5
RMS_EPS = 1e-5

TOKEN_TILE = 512
HG_CHUNK = 128
GROUP = HG_DK
N_GROUPS = D_MIX // GROUP
OUT_ROW_SPLIT = 2
SAFE_BLOCK = 8
MAX_EXPONENT = 80.0
VMEM_LIMIT_BYTES = 56 * 1024 * 1024

F32 = jnp.float32
BF16 = jnp.bfloat16
LOG2E = math.log2(math.e)


def _layer_norm_rows(v, g, b):
    mu = jnp.mean(v, axis=-1, keepdims=True)
    d = v - mu
    var = jnp.mean(d * d, axis=-1, keepdims=True)
    return d * lax.rsqrt(var + LN_EPS) * g + b


def _sigmoid(v):
    return 1.0 / (1.0 + jnp.exp(-v))


def _dot(a, b):
    return jnp.dot(a, b, preferred_element_type=F32)


def _dot_nt(a, b):
    return lax.dot_general(a, b, (((1,), (1,)), ((), ())), preferred_element_type=F32)


def _dot_tn(a, b):
    return lax.dot_general(a, b, (((0,), (0,)), ((), ())), preferred_element_type=F32)


def _mem_kv_kernel(mem_ref, w_ref, g_ref, b_ref, kt_ref, v_ref):
    m = _layer_norm_rows(mem_ref[0], g_ref[...], b_ref[...])
    kv = _dot(m.astype(BF16), w_ref[...])
    kt_ref[0] = kv[:, :D_XA].T.astype(BF16)
    v_ref[0] = kv[:, D_XA:].astype(BF16)


def _mem_kv(mem, w_mem_kv, g, b):
    bsz = mem.shape[0]
    return pl.pallas_call(
        _mem_kv_kernel,
        grid=(bsz,),
        in_specs=[
            pl.BlockSpec((1, N_MEM, D_MODEL), lambda i: (i, 0, 0)),
            pl.BlockSpec((D_MODEL, 2 * D_XA), lambda i: (0, 0)),
            pl.BlockSpec((1, D_MODEL), lambda i: (0, 0)),
            pl.BlockSpec((1, D_MODEL), lambda i: (0, 0)),
        ],
        out_specs=[
            pl.BlockSpec((1, D_XA, N_MEM), lambda i: (i, 0, 0)),
            pl.BlockSpec((1, N_MEM, D_XA), lambda i: (i, 0, 0)),
        ],
        out_shape=[
            jax.ShapeDtypeStruct((bsz, D_XA, N_MEM), BF16),
            jax.ShapeDtypeStruct((bsz, N_MEM, D_XA), BF16),
        ],
        compiler_params=pltpu.CompilerParams(
            dimension_semantics=("arbitrary",), vmem_limit_bytes=VMEM_LIMIT_BYTES),
        name="mem_kv",
    )(mem, w_mem_kv, g, b)


def _mix_proj(xb, w_in_ref, g, parts):
    w = [w_in_ref[:, p * D_MIX + g * GROUP:p * D_MIX + (g + 1) * GROUP] for p in parts]
    return _dot(xb, jnp.concatenate(w, axis=1))


def _forget_lower_bound(lbl_ref, lb_index):
    lg = lbl_ref[...]
    lg = jnp.exp(lg - jnp.max(lg, axis=0, keepdims=True))
    return jnp.sum(lg[:lb_index + 1], axis=0, keepdims=True) / jnp.sum(lg, axis=0, keepdims=True)


def _cross_attn_tasks(xb, w_in_ref, kt_ref, v_ref, cat_ref, zx_ref):
    c = XA_DH ** -0.5 * LOG2E
    scores = {}

    def queries(n):
        cols = slice(n * 2 * XA_DH, (n + 1) * 2 * XA_DH)
        zx_ref[:, cols] = _dot(xb, w_in_ref[:, 4 * D_MIX + cols.start:4 * D_MIX + cols.stop]).astype(BF16)

    def logits(h):
        hs = slice(h * XA_DH, (h + 1) * XA_DH)
        scores[h] = _dot(zx_ref[:, hs], kt_ref[0, hs, :])

    def read(h):
        hs = slice(h * XA_DH, (h + 1) * XA_DH)
        s = scores.pop(h)
        e = jnp.exp2((s - jnp.max(s, axis=-1, keepdims=True)) * c)
        l = jnp.sum(e, axis=-1, keepdims=True)
        o = _dot(e.astype(BF16), v_ref[0, :, hs]) * (1.0 / l)
        cat_ref[:, D_MIX + h * XA_DH:D_MIX + (h + 1) * XA_DH] = o.astype(BF16)

    tasks = [functools.partial(queries, 0), functools.partial(queries, 1)]
    for h in range(XA_HEADS):
        tasks += [functools.partial(logits, h), functools.partial(read, h)]
    return tasks


def _out_proj_norm(x_ref, cat_ref, w_out_ref, g_ref, b_ref, o_ref):
    t = TOKEN_TILE // OUT_ROW_SPLIT
    ys = [_dot(cat_ref[r * t:(r + 1) * t, :], w_out_ref[...]) for r in range(OUT_ROW_SPLIT)]
    for r, y in enumerate(ys):
        rows = slice(r * t, (r + 1) * t)
        o_ref[0, rows, :] = _layer_norm_rows(DN_ALPHA * x_ref[0, rows, :] + y, g_ref[...], b_ref[...])


def _hgrn_layer_kernel(x_ref, w_in_ref, w_out_ref, g_ref, b_ref, kt_ref, v_ref,
                       lbl_ref, nw_ref, o_ref, cat_ref, zx_ref, st_ref, *z_refs, lb_index):
    @pl.when(pl.program_id(1) == 0)
    def _():
        st_ref[...] = jnp.zeros_like(st_ref)

    xb = x_ref[0].astype(BF16)
    lb_all = _forget_lower_bound(lbl_ref, lb_index)
    nw = nw_ref[...]

    c_len = HG_CHUNK
    n_chunks = TOKEN_TILE // c_len
    n_items = HG_HEADS * n_chunks
    row = lax.broadcasted_iota(jnp.int32, (c_len, c_len), 0)
    col = lax.broadcasted_iota(jnp.int32, (c_len, c_len), 1)
    causal = col <= row
    tri = causal.astype(BF16)
    half = 2 * GROUP

    def proj_piece(q):
        h, n = divmod(q, 2)
        z_refs[h][:, n * half:(n + 1) * half] = _mix_proj(xb, w_in_ref, h, (2 * n, 2 * n + 1))

    def stage_a(i):
        h, ci = divmod(i, n_chunks)
        rows = slice(ci * c_len, (ci + 1) * c_len)
        lb = lb_all[:, h * HG_DK:(h + 1) * HG_DK]
        zf = z_refs[h][rows, HG_DK:2 * HG_DK]
        f = lb + (1.0 - lb) * _sigmoid(zf)
        logf = jnp.log(f)
        hi = logf.astype(BF16)
        lo = (logf - hi.astype(F32)).astype(BF16)
        bb = _dot(tri, jnp.concatenate([hi, lo], axis=1))
        return dict(kk=1.0 - f, bb=bb)

    def stage_b(i, v):
        h, ci = divmod(i, n_chunks)
        rows = slice(ci * c_len, (ci + 1) * c_len)
        zq = z_refs[h][rows, 0:HG_DK]
        zi = z_refs[h][rows, 2 * HG_DK:3 * HG_DK]
        bc = v["bb"][:, :HG_DK] + v["bb"][:, HG_DK:]
        b_last = bc[c_len - 1:c_len, :]
        mid = 0.5 * b_last
        e_mid = jnp.exp(mid)
        qh = zq * _sigmoid(zq) * jnp.exp(bc - mid)
        kh = v["kk"] * jnp.exp(mid - bc)
        q_in = (qh * e_mid).astype(BF16)
        k_dec = (kh * e_mid).astype(BF16)
        vt = zi.T.astype(BF16)
        return dict(q_in=q_in, vt=vt, decay=jnp.exp(b_last),
                    scores=_dot_nt(qh.astype(BF16), kh.astype(BF16)), kv=_dot(vt, k_dec))

    def stage_c(v, st):
        a = jnp.where(causal, v["scores"], 0.0).astype(BF16)
        o = _dot_nt(jnp.concatenate([v["q_in"], a], axis=1),
                    jnp.concatenate([st.astype(BF16), v["vt"]], axis=1))
        return o, st * v["decay"] + v["kv"]

    def stage_d(i, o):
        h, ci = divmod(i, n_chunks)
        rows = slice(ci * c_len, (ci + 1) * c_len)
        zg = z_refs[h][rows, 3 * HG_DK:4 * HG_DK]
        o = o * lax.rsqrt(jnp.mean(o * o, axis=-1, keepdims=True) + RMS_EPS) * nw
        cat_ref[rows, h * HG_DK:(h + 1) * HG_DK] = (o * (zg * _sigmoid(zg))).astype(BF16)

    proj_lead = 6
    n_pieces = 2 * HG_HEADS
    attn_tasks = _cross_attn_tasks(xb, w_in_ref, kt_ref, v_ref, cat_ref, zx_ref)
    emitted = 0
    while 2 * emitted - proj_lead < -3:
        proj_piece(emitted)
        emitted += 1
    va, vb, vo, st = {}, {}, {}, None
    for s in range(-3, n_items):
        if emitted < n_pieces and 2 * emitted - proj_lead <= s:
            proj_piece(emitted)
            emitted += 1
        elif attn_tasks and s >= 0:
            attn_tasks.pop(0)()
        if 0 <= s + 3 < n_items:
            va[s + 3] = stage_a(s + 3)
        if 0 <= s + 2 < n_items:
            vb[s + 2] = stage_b(s + 2, va.pop(s + 2))
        if 0 <= s + 1 < n_items:
            h, ci = divmod(s + 1, n_chunks)
            if ci == 0:
                st = st_ref[h]
            vo[s + 1], st = stage_c(vb.pop(s + 1), st)
            if ci == n_chunks - 1:
                st_ref[h] = st
        if 0 <= s:
            stage_d(s, vo.pop(s))
    while attn_tasks:
        attn_tasks.pop(0)()

    _out_proj_norm(x_ref, cat_ref, w_out_ref, g_ref, b_ref, o_ref)


def _hgrn_safe_layer_kernel(x_ref, w_in_ref, w_out_ref, g_ref, b_ref, kt_ref, v_ref,
                            lbl_ref, nw_ref, o_ref, cat_ref, zx_ref, st_ref, z_ref, *, lb_index):
    @pl.when(pl.program_id(1) == 0)
    def _():
        st_ref[...] = jnp.zeros_like(st_ref)

    xb = x_ref[0].astype(BF16)
    for p in range(4):
        for g in range(0, N_GROUPS, 2):
            cols = slice(p * D_MIX + g * GROUP, p * D_MIX + (g + 2) * GROUP)
            z_ref[:, cols] = _dot(xb, w_in_ref[:, cols])
    for task in _cross_attn_tasks(xb, w_in_ref, kt_ref, v_ref, cat_ref, zx_ref):
        task()

    lb_all = _forget_lower_bound(lbl_ref, lb_index)
    nw = nw_ref[...]
    n = SAFE_BLOCK
    row = lax.broadcasted_iota(jnp.int32, (n, HG_DK), 0)
    shifts = [1 << j for j in range(n.bit_length() - 1)]

    def block(rows, h):
        hs = slice(h * HG_DK, (h + 1) * HG_DK)
        zq = z_ref[rows, h * HG_DK:(h + 1) * HG_DK]
        zf = z_ref[rows, D_MIX + h * HG_DK:D_MIX + (h + 1) * HG_DK]
        v = z_ref[rows, 2 * D_MIX + h * HG_DK:2 * D_MIX + (h + 1) * HG_DK]
        zg = z_ref[rows, 3 * D_MIX + h * HG_DK:3 * D_MIX + (h + 1) * HG_DK]
        lb = lb_all[:, hs]
        q = zq * _sigmoid(zq)
        f = lb + (1.0 - lb) * _sigmoid(zf)
        k = 1.0 - f
        pre = f
        suf = jnp.where(row < n - 1, pltpu.roll(f, n - 1, 0), 1.0)
        for sh in shifts:
            pre = pre * jnp.where(row >= sh, pltpu.roll(pre, sh, 0), 1.0)
            suf = suf * jnp.where(row + sh < n, pltpu.roll(suf, n - sh, 0), 1.0)
        st = st_ref[h]
        o = _dot_nt(q * pre, st)
        dec = None
        for lag in range(n):
            dec = jnp.ones_like(f) if lag == 0 else dec * pltpu.roll(f, lag - 1, 0)
            k_s = k if lag == 0 else pltpu.roll(k, lag, 0)
            v_s = v if lag == 0 else pltpu.roll(v, lag, 0)
            a = jnp.sum(q * k_s * dec, axis=-1, keepdims=True)
            o = o + jnp.where(row >= lag, a * v_s, 0.0)
        st_ref[h] = st * pre[n - 1:n, :] + _dot_tn(v, k * suf)
        o = o * lax.rsqrt(jnp.mean(o * o, axis=-1, keepdims=True) + RMS_EPS) * nw
        return o * (zg * _sigmoid(zg))

    def block_pair(bi, carry):
        r0 = pl.multiple_of(bi * 2 * n, 2 * n)
        for h in range(HG_HEADS):
            o = [block(pl.ds(r0 + j * n, n), h) for j in range(2)]
            cat_ref[pl.ds(r0, 2 * n), h * HG_DK:(h + 1) * HG_DK] = jnp.concatenate(o, axis=0).astype(BF16)
        return carry

    lax.fori_loop(0, TOKEN_TILE // (2 * n), block_pair, 0)
    _out_proj_norm(x_ref, cat_ref, w_out_ref, g_ref, b_ref, o_ref)


def _conv_layer_kernel(x_ref, w_in_ref, w_out_ref, g_ref, b_ref, kt_ref, v_ref,
                       cw_ref, o_ref, cat_ref, zx_ref, u_ref):
    t = TOKEN_TILE

    @pl.when(pl.program_id(1) == 0)
    def _():
        u_ref[0:8, :] = jnp.zeros((8, D_MIX), F32)

    xb = x_ref[0].astype(BF16)
    cw = cw_ref[...]

    def mix(g, z):
        gs = slice(g * GROUP, (g + 1) * GROUP)
        u = z[:, GROUP:2 * GROUP] * z[:, 2 * GROUP:3 * GROUP]
        u_ref[8:8 + t, gs] = u
        conv = (cw[0:1, gs] * u_ref[8 - 2:8 - 2 + t, gs] + cw[1:2, gs] * u_ref[8 - 1:8 - 1 + t, gs]
                + cw[2:3, gs] * u)
        u_ref[0:8, gs] = u_ref[t:t + 8, gs]
        zg = z[:, 3 * GROUP:4 * GROUP]
        cat_ref[:, gs] = (z[:, 0:GROUP] * conv * (zg * _sigmoid(zg))).astype(BF16)

    attn_tasks = _cross_attn_tasks(xb, w_in_ref, kt_ref, v_ref, cat_ref, zx_ref)
    z_next = _mix_proj(xb, w_in_ref, 0, (0, 1, 2, 3))
    for g in range(N_GROUPS):
        z = z_next
        if g + 1 < N_GROUPS:
            z_next = _mix_proj(xb, w_in_ref, g + 1, (0, 1, 2, 3))
        if attn_tasks:
            attn_tasks.pop(0)()
        mix(g, z)
    while attn_tasks:
        attn_tasks.pop(0)()

    _out_proj_norm(x_ref, cat_ref, w_out_ref, g_ref, b_ref, o_ref)


def _layer_call(body, name, x, w_in, w_out, g, b, kt, v, *, extra, extra_scratch):
    bsz, slen, _ = x.shape
    t = TOKEN_TILE
    const2 = lambda i, j: (0, 0)
    single = pl.Buffered(1)
    in_specs = [
        pl.BlockSpec((1, t, D_MODEL), lambda i, j: (i, j, 0)),
        pl.BlockSpec((D_MODEL, D_IN), const2, pipeline_mode=single),
        pl.BlockSpec((D_CAT, D_MODEL), const2, pipeline_mode=single),
        pl.BlockSpec((1, D_MODEL), const2),
        pl.BlockSpec((1, D_MODEL), const2),
        pl.BlockSpec((1, D_XA, N_MEM), lambda i, j: (i, 0, 0)),
        pl.BlockSpec((1, N_MEM, D_XA), lambda i, j: (i, 0, 0)),
    ] + [pl.BlockSpec(e.shape, const2) for e in extra]
    return pl.pallas_call(
        body,
        grid=(bsz, slen // t),
        in_specs=in_specs,
        out_specs=pl.BlockSpec((1, t, D_MODEL), lambda i, j: (i, j, 0)),
        out_shape=jax.ShapeDtypeStruct((bsz, slen, D_MODEL), F32),
        scratch_shapes=[pltpu.VMEM((t, D_CAT), BF16), pltpu.VMEM((t, D_XA), BF16)] + extra_scratch,
        compiler_params=pltpu.CompilerParams(
            dimension_semantics=("arbitrary", "arbitrary"),
            vmem_limit_bytes=VMEM_LIMIT_BYTES),
        name=name,
    )(x, w_in, w_out, g, b, kt, v, *extra)


def kernel(x, mem, w_in, w_out, ln_g, ln_b, hgrn_lb_logits, hgrn_norm_w, conv_w,
           mem_ln_g, mem_ln_b, w_mem_kv):
    bsz, slen, d_model = x.shape
    assert d_model == D_MODEL and slen % TOKEN_TILE == 0
    assert w_in.shape == (DEPTH, D_MODEL, D_IN) and w_out.shape == (DEPTH, D_CAT, D_MODEL)
    assert mem.shape == (bsz, N_MEM, D_MODEL)

    kt, v = _mem_kv(mem, w_mem_kv.astype(BF16), mem_ln_g.reshape(1, -1), mem_ln_b.reshape(1, -1))
    w_in_b = w_in.astype(BF16)
    w_out_b = w_out.astype(BF16)

    for layer in range(DEPTH):
        j = layer // 2
        g = ln_g[layer].reshape(1, -1)
        b = ln_b[layer].reshape(1, -1)
        args = (x, w_in_b[layer], w_out_b[layer], g, b, kt, v)
        if layer % 2 == 0:
            extra = [hgrn_lb_logits, hgrn_norm_w[j].reshape(1, -1)]
            state = pltpu.VMEM((HG_HEADS, HG_DK, HG_DK), F32)
            fast = functools.partial(
                _layer_call, functools.partial(_hgrn_layer_kernel, lb_index=j), "hgrn_layer",
                extra=extra, extra_scratch=[state] + [
                    pltpu.VMEM((TOKEN_TILE, 4 * GROUP), F32) for _ in range(HG_HEADS)])
            safe = functools.partial(
                _layer_call, functools.partial(_hgrn_safe_layer_kernel, lb_index=j), "hgrn_safe_layer",
                extra=extra, extra_scratch=[state, pltpu.VMEM((TOKEN_TILE, 4 * D_MIX), F32)])
            log_lb = jax.nn.log_softmax(hgrn_lb_logits.astype(F32), axis=0)
            log_lb = jax.nn.logsumexp(log_lb[:j + 1], axis=0)
            fast_ok = 0.5 * HG_CHUNK * jnp.max(-log_lb) <= MAX_EXPONENT
            x = lax.cond(fast_ok, fast, safe, *args)
        else:
            x = _layer_call(_conv_layer_kernel, "conv_layer", *args, extra=[conv_w[j]],
                            extra_scratch=[pltpu.VMEM((TOKEN_TILE + 8, D_MIX), F32)])
    return x
```

```python
import functools
import math

import jax
import jax.numpy as jnp
from jax import lax
from jax.experimental import pallas as pl
from jax.experimental.pallas import tpu as pltpu

D_MODEL = 1024
N_MEM = 256
HG_HEADS = 8
HG_DK = 128
D_MIX = 1024
CONV_WIDTH = 3
XA_HEADS = 4
XA_DH = 128
D_XA = XA_HEADS * XA_DH
D_IN = 4 * D_MIX + D_XA
D_CAT = D_MIX + D_XA
DEPTH = 2
DN_ALPHA = (2 * DEPTH) ** 0.25
LN_EPS = 1e-5
RMS_EPS = 1e-5

SUB_TILE = 512
TOKEN_TILE = 2 * SUB_TILE
MEM_BATCH = 4
HG_CHUNK = 128
GROUP = HG_DK
N_GROUPS = D_MIX // GROUP
OUT_ROW_SPLIT = 2
SAFE_BLOCK = 8
MAX_EXPONENT = 80.0
VMEM_LIMIT_BYTES = 58 * 1024 * 1024

F32 = jnp.float32
BF16 = jnp.bfloat16
LOG2E = math.log2(math.e)


def _layer_norm_rows(v, g, b):
    mu = jnp.mean(v, axis=-1, keepdims=True)
    d = v - mu
    var = jnp.mean(d * d, axis=-1, keepdims=True)
    return d * lax.rsqrt(var + LN_EPS) * g + b


def _sigmoid(v):
    return 1.0 / (1.0 + jnp.exp(-v))


def _dot(a, b):
    return jnp.dot(a, b, preferred_element_type=F32)


def _dot_nt(a, b):
    return lax.dot_general(a, b, (((1,), (1,)), ((), ())), preferred_element_type=F32)


def _dot_tn(a, b):
    return lax.dot_general(a, b, (((0,), (0,)), ((), ())), preferred_element_type=F32)


def _mem_kv_kernel(mem_ref, w_ref, g_ref, b_ref, kt_ref, v_ref):
    m = _layer_norm_rows(mem_ref[...].reshape(MEM_BATCH * N_MEM, D_MODEL), g_ref[...], b_ref[...])
    kv = _dot(m.astype(BF16), w_ref[...])
    for i in range(MEM_BATCH):
        rows = slice(i * N_MEM, (i + 1) * N_MEM)
        kt_ref[i] = kv[rows, :D_XA].T.astype(BF16)
        v_ref[i] = kv[rows, D_XA:].astype(BF16)


def _mem_kv(mem, w_mem_kv, g, b):
    bsz = mem.shape[0]
    return pl.pallas_call(
        _mem_kv_kernel,
        grid=(bsz // MEM_BATCH,),
        in_specs=[
            pl.BlockSpec((MEM_BATCH, N_MEM, D_MODEL), lambda i: (i, 0, 0)),
            pl.BlockSpec((D_MODEL, 2 * D_XA), lambda i: (0, 0)),
            pl.BlockSpec((1, D_MODEL), lambda i: (0, 0)),
            pl.BlockSpec((1, D_MODEL), lambda i: (0, 0)),
        ],
        out_specs=[
            pl.BlockSpec((MEM_BATCH, D_XA, N_MEM), lambda i: (i, 0, 0)),
            pl.BlockSpec((MEM_BATCH, N_MEM, D_XA), lambda i: (i, 0, 0)),
        ],
        out_shape=[
            jax.ShapeDtypeStruct((bsz, D_XA, N_MEM), BF16),
            jax.ShapeDtypeStruct((bsz, N_MEM, D_XA), BF16),
        ],
        compiler_params=pltpu.CompilerParams(
            dimension_semantics=("arbitrary",), vmem_limit_bytes=VMEM_LIMIT_BYTES),
        name="mem_kv",
    )(mem, w_mem_kv, g, b)


def _sub_rows(r0):
    return slice(r0, r0 + SUB_TILE)


def _mix_proj(xb, w_in_ref, g, parts):
    w = [w_in_ref[:, p * D_MIX + g * GROUP:p * D_MIX + (g + 1) * GROUP] for p in parts]
    return _dot(xb, jnp.concatenate(w, axis=1))


def _forget_lower_bound(lbl_ref, lb_index):
    lg = lbl_ref[...]
    lg = jnp.exp(lg - jnp.max(lg, axis=0, keepdims=True))
    return jnp.sum(lg[:lb_index + 1], axis=0, keepdims=True) / jnp.sum(lg, axis=0, keepdims=True)


def _cross_attn_tasks(xb, r0, w_in_ref, kt_ref, v_ref, cat_ref, zx_ref):
    c = XA_DH ** -0.5 * LOG2E
    rows = _sub_rows(r0)
    scores = {}

    def queries(n):
        cols = slice(n * 2 * XA_DH, (n + 1) * 2 * XA_DH)
        zx_ref[rows, cols] = _dot(
            xb, w_in_ref[:, 4 * D_MIX + cols.start:4 * D_MIX + cols.stop]).astype(BF16)

    def logits(h):
        hs = slice(h * XA_DH, (h + 1) * XA_DH)
        scores[h] = _dot(zx_ref[rows, hs], kt_ref[0, hs, :])

    def read(h):
        hs = slice(h * XA_DH, (h + 1) * XA_DH)
        s = scores.pop(h)
        e = jnp.exp2((s - jnp.max(s, axis=-1, keepdims=True)) * c)
        l = jnp.sum(e, axis=-1, keepdims=True)
        o = _dot(e.astype(BF16), v_ref[0, :, hs]) * (1.0 / l)
        cat_ref[rows, D_MIX + h * XA_DH:D_MIX + (h + 1) * XA_DH] = o.astype(BF16)

    tasks = [functools.partial(queries, 0), functools.partial(queries, 1)]
    for h in range(XA_HEADS):
        tasks += [functools.partial(logits, h), functools.partial(read, h)]
    return tasks


def _out_proj_norm(r0, x_ref, cat_ref, w_out_ref, g_ref, b_ref, o_ref):
    t = SUB_TILE // OUT_ROW_SPLIT
    slabs = [slice(r0 + r * t, r0 + (r + 1) * t) for r in range(OUT_ROW_SPLIT)]
    ys = [_dot(cat_ref[rows, :], w_out_ref[...]) for rows in slabs]
    for rows, y in zip(slabs, ys):
        o_ref[0, rows, :] = _layer_norm_rows(DN_ALPHA * x_ref[0, rows, :] + y, g_ref[...], b_ref[...])


def _hgrn_layer_kernel(x_ref, w_in_ref, w_out_ref, g_ref, b_ref, kt_ref, v_ref,
                       lbl_ref, nw_ref, o_ref, cat_ref, zx_ref, st_ref, *z_refs, lb_index):
    @pl.when(pl.program_id(1) == 0)
    def _():
        st_ref[...] = jnp.zeros_like(st_ref)

    n_sub = x_ref.shape[1] // SUB_TILE
    xbs = [x_ref[0, _sub_rows(k * SUB_TILE), :].astype(BF16) for k in range(n_sub)]
    lb_all = _forget_lower_bound(lbl_ref, lb_index)
    nw = nw_ref[...]

    c_len = HG_CHUNK
    n_chunks = SUB_TILE // c_len
    sub_items = HG_HEADS * n_chunks
    n_items = n_sub * sub_items
    row = lax.broadcasted_iota(jnp.int32, (c_len, c_len), 0)
    col = lax.broadcasted_iota(jnp.int32, (c_len, c_len), 1)
    causal = col <= row
    tri = causal.astype(BF16)
    half = 2 * GROUP

    def item(i):
        k, rem = divmod(i, sub_items)
        h, ci = divmod(rem, n_chunks)
        zrows = slice(ci * c_len, (ci + 1) * c_len)
        return h, ci, zrows, slice(k * SUB_TILE + zrows.start, k * SUB_TILE + zrows.stop)

    def proj_piece(q):
        k, rem = divmod(q, 2 * HG_HEADS)
        h, n = divmod(rem, 2)
        z_refs[h][:, n * half:(n + 1) * half] = _mix_proj(xbs[k], w_in_ref, h, (2 * n, 2 * n + 1))

    def stage_a(i):
        h, _, zrows, _ = item(i)
        lb = lb_all[:, h * HG_DK:(h + 1) * HG_DK]
        zf = z_refs[h][zrows, HG_DK:2 * HG_DK]
        f = lb + (1.0 - lb) * _sigmoid(zf)
        logf = jnp.log(f)
        hi = logf.astype(BF16)
        lo = (logf - hi.astype(F32)).astype(BF16)
        bb = _dot(tri, jnp.concatenate([hi, lo], axis=1))
        return dict(kk=1.0 - f, bb=bb)

    def stage_b(i, v):
        h, _, zrows, _ = item(i)
        zq = z_refs[h][zrows, 0:HG_DK]
        zi = z_refs[h][zrows, 2 * HG_DK:3 * HG_DK]
        bc = v["bb"][:, :HG_DK] + v["bb"][:, HG_DK:]
        b_last = bc[c_len - 1:c_len, :]
        mid = 0.5 * b_last
        e_mid = jnp.exp(mid)
        qh = zq * _sigmoid(zq) * jnp.exp(bc - mid)
        kh = v["kk"] * jnp.exp(mid - bc)
        q_in = (qh * e_mid).astype(BF16)
        k_dec = (kh * e_mid).astype(BF16)
        vt = zi.T.astype(BF16)
        return dict(q_in=q_in, vt=vt, decay=jnp.exp(b_last),
                    scores=_dot_nt(qh.astype(BF16), kh.astype(BF16)), kv=_dot(vt, k_dec))

    def stage_c(v, st):
        a = jnp.where(causal, v["scores"], 0.0).astype(BF16)
        o = _dot_nt(jnp.concatenate([v["q_in"], a], axis=1),
                    jnp.concatenate([st.astype(BF16), v["vt"]], axis=1))
        return o, st * v["decay"] + v["kv"]

    def stage_d(i, o):
        h, _, zrows, rows = item(i)
        zg = z_refs[h][zrows, 3 * HG_DK:4 * HG_DK]
        o = o * lax.rsqrt(jnp.mean(o * o, axis=-1, keepdims=True) + RMS_EPS) * nw
        cat_ref[rows, h * HG_DK:(h + 1) * HG_DK] = (o * (zg * _sigmoid(zg))).astype(BF16)

    proj_lead = 6
    n_pieces = n_sub * 2 * HG_HEADS
    attn_tasks = [_cross_attn_tasks(xbs[k], k * SUB_TILE, w_in_ref, kt_ref, v_ref, cat_ref, zx_ref)
                  for k in range(n_sub)]
    emitted = 0
    while 2 * emitted - proj_lead < -3:
        proj_piece(emitted)
        emitted += 1
    va, vb, vo, st = {}, {}, {}, None
    for s in range(-3, n_items):
        if emitted < n_pieces and 2 * emitted - proj_lead <= s:
            proj_piece(emitted)
            emitted += 1
        elif s >= 0 and attn_tasks[s // sub_items]:
            attn_tasks[s // sub_items].pop(0)()
        if 0 <= s + 3 < n_items:
            va[s + 3] = stage_a(s + 3)
        if 0 <= s + 2 < n_items:
            vb[s + 2] = stage_b(s + 2, va.pop(s + 2))
        if 0 <= s + 1 < n_items:
            h, ci, _, _ = item(s + 1)
            if ci == 0:
                st = st_ref[h]
            vo[s + 1], st = stage_c(vb.pop(s + 1), st)
            if ci == n_chunks - 1:
                st_ref[h] = st
        if 0 <= s:
            stage_d(s, vo.pop(s))
            k, rem = divmod(s + 1, sub_items)
            if rem == 0:
                while attn_tasks[k - 1]:
                    attn_tasks[k - 1].pop(0)()
                _out_proj_norm((k - 1) * SUB_TILE, x_ref, cat_ref, w_out_ref, g_ref, b_ref, o_ref)


def _hgrn_safe_layer_kernel(x_ref, w_in_ref, w_out_ref, g_ref, b_ref, kt_ref, v_ref,
                            lbl_ref, nw_ref, o_ref, cat_ref, zx_ref, st_ref, z_ref, *, lb_index):
    @pl.when(pl.program_id(1) == 0)
    def _():
        st_ref[...] = jnp.zeros_like(st_ref)

    xb = x_ref[0].astype(BF16)
    for p in range(4):
        for g in range(0, N_GROUPS, 2):
            cols = slice(p * D_MIX + g * GROUP, p * D_MIX + (g + 2) * GROUP)
            z_ref[:, cols] = _dot(xb, w_in_ref[:, cols])
    for task in _cross_attn_tasks(xb, 0, w_in_ref, kt_ref, v_ref, cat_ref, zx_ref):
        task()

    lb_all = _forget_lower_bound(lbl_ref, lb_index)
    nw = nw_ref[...]
    n = SAFE_BLOCK
    row = lax.broadcasted_iota(jnp.int32, (n, HG_DK), 0)
    shifts = [1 << j for j in range(n.bit_length() - 1)]

    def block(rows, h):
        hs = slice(h * HG_DK, (h + 1) * HG_DK)
        zq = z_ref[rows, h * HG_DK:(h + 1) * HG_DK]
        zf = z_ref[rows, D_MIX + h * HG_DK:D_MIX + (h + 1) * HG_DK]
        v = z_ref[rows, 2 * D_MIX + h * HG_DK:2 * D_MIX + (h + 1) * HG_DK]
        zg = z_ref[rows, 3 * D_MIX + h * HG_DK:3 * D_MIX + (h + 1) * HG_DK]
        lb = lb_all[:, hs]
        q = zq * _sigmoid(zq)
        f = lb + (1.0 - lb) * _sigmoid(zf)
        k = 1.0 - f
        pre = f
        suf = jnp.where(row < n - 1, pltpu.roll(f, n - 1, 0), 1.0)
        for sh in shifts:
            pre = pre * jnp.where(row >= sh, pltpu.roll(pre, sh, 0), 1.0)
            suf = suf * jnp.where(row + sh < n, pltpu.roll(suf, n - sh, 0), 1.0)
        st = st_ref[h]
        o = _dot_nt(q * pre, st)
        dec = None
        for lag in range(n):
            dec = jnp.ones_like(f) if lag == 0 else dec * pltpu.roll(f, lag - 1, 0)
            k_s = k if lag == 0 else pltpu.roll(k, lag, 0)
            v_s = v if lag == 0 else pltpu.roll(v, lag, 0)
            a = jnp.sum(q * k_s * dec, axis=-1, keepdims=True)
            o = o + jnp.where(row >= lag, a * v_s, 0.0)
        st_ref[h] = st * pre[n - 1:n, :] + _dot_tn(v, k * suf)
        o = o * lax.rsqrt(jnp.mean(o * o, axis=-1, keepdims=True) + RMS_EPS) * nw
        return o * (zg * _sigmoid(zg))

    def block_pair(bi, carry):
        r0 = pl.multiple_of(bi * 2 * n, 2 * n)
        for h in range(HG_HEADS):
            o = [block(pl.ds(r0 + j * n, n), h) for j in range(2)]
            cat_ref[pl.ds(r0, 2 * n), h * HG_DK:(h + 1) * HG_DK] = jnp.concatenate(o, axis=0).astype(BF16)
        return carry

    lax.fori_loop(0, SUB_TILE // (2 * n), block_pair, 0)
    _out_proj_norm(0, x_ref, cat_ref, w_out_ref, g_ref, b_ref, o_ref)


def _conv_layer_kernel(x_ref, w_in_ref, w_out_ref, g_ref, b_ref, kt_ref, v_ref,
                       cw_ref, o_ref, cat_ref, zx_ref, u_ref):
    @pl.when(pl.program_id(1) == 0)
    def _():
        u_ref[0:8, :] = jnp.zeros((8, D_MIX), F32)

    tile = x_ref.shape[1]
    n_sub = tile // SUB_TILE
    xbs = [x_ref[0, _sub_rows(k * SUB_TILE), :].astype(BF16) for k in range(n_sub)]
    cw = cw_ref[...]

    def mix(k, g, z):
        gs = slice(g * GROUP, (g + 1) * GROUP)
        r0 = 8 + k * SUB_TILE
        u = z[:, GROUP:2 * GROUP] * z[:, 2 * GROUP:3 * GROUP]
        u_ref[r0:r0 + SUB_TILE, gs] = u
        conv = (cw[0:1, gs] * u_ref[r0 - 2:r0 - 2 + SUB_TILE, gs]
                + cw[1:2, gs] * u_ref[r0 - 1:r0 - 1 + SUB_TILE, gs] + cw[2:3, gs] * u)
        if k == n_sub - 1:
            u_ref[0:8, gs] = u_ref[tile:tile + 8, gs]
        zg = z[:, 3 * GROUP:4 * GROUP]
        cat_ref[_sub_rows(k * SUB_TILE), gs] = (z[:, 0:GROUP] * conv * (zg * _sigmoid(zg))).astype(BF16)

    attn_tasks = [_cross_attn_tasks(xbs[k], k * SUB_TILE, w_in_ref, kt_ref, v_ref, cat_ref, zx_ref)
                  for k in range(n_sub)]
    n_steps = n_sub * N_GROUPS
    z_next = _mix_proj(xbs[0], w_in_ref, 0, (0, 1, 2, 3))
    for s in range(n_steps):
        k, g = divmod(s, N_GROUPS)
        z = z_next
        if s + 1 < n_steps:
            k1, g1 = divmod(s + 1, N_GROUPS)
            z_next = _mix_proj(xbs[k1], w_in_ref, g1, (0, 1, 2, 3))
        for _ in range(-(-len(attn_tasks[k]) // (N_GROUPS - g))):
            attn_tasks[k].pop(0)()
        mix(k, g, z)
        if g == N_GROUPS - 1:
            _out_proj_norm(k * SUB_TILE, x_ref, cat_ref, w_out_ref, g_ref, b_ref, o_ref)


def _layer_call(body, name, x, w_in, w_out, g, b, kt, v, *, layer, tile, extra, extra_scratch):
    bsz, slen, _ = x.shape
    const2 = lambda i, j: (0, 0)
    weights = lambda i, j: (layer, 0, 0)
    single = pl.Buffered(1)
    in_specs = [
        pl.BlockSpec((1, tile, D_MODEL), lambda i, j: (i, j, 0)),
        pl.BlockSpec((None, D_MODEL, D_IN), weights, pipeline_mode=single),
        pl.BlockSpec((None, D_CAT, D_MODEL), weights, pipeline_mode=single),
        pl.BlockSpec((1, D_MODEL), const2),
        pl.BlockSpec((1, D_MODEL), const2),
        pl.BlockSpec((1, D_XA, N_MEM), lambda i, j: (i, 0, 0)),
        pl.BlockSpec((1, N_MEM, D_XA), lambda i, j: (i, 0, 0)),
    ] + [pl.BlockSpec(e.shape, const2) for e in extra]
    return pl.pallas_call(
        body,
        grid=(bsz, slen // tile),
        in_specs=in_specs,
        out_specs=pl.BlockSpec((1, tile, D_MODEL), lambda i, j: (i, j, 0)),
        out_shape=jax.ShapeDtypeStruct((bsz, slen, D_MODEL), F32),
        scratch_shapes=[pltpu.VMEM((tile, D_CAT), BF16), pltpu.VMEM((tile, D_XA), BF16)] + extra_scratch,
        compiler_params=pltpu.CompilerParams(
            dimension_semantics=("arbitrary", "arbitrary"),
            vmem_limit_bytes=VMEM_LIMIT_BYTES),
        name=name,
    )(x, w_in, w_out, g, b, kt, v, *extra)


def kernel(x, mem, w_in, w_out, ln_g, ln_b, hgrn_lb_logits, hgrn_norm_w, conv_w,
           mem_ln_g, mem_ln_b, w_mem_kv):
    bsz, slen, d_model = x.shape
    assert d_model == D_MODEL and slen % TOKEN_TILE == 0 and bsz % MEM_BATCH == 0
    assert w_in.shape == (DEPTH, D_MODEL, D_IN) and w_out.shape == (DEPTH, D_CAT, D_MODEL)
    assert mem.shape == (bsz, N_MEM, D_MODEL)

    kt, v = _mem_kv(mem, w_mem_kv.astype(BF16), mem_ln_g.reshape(1, -1), mem_ln_b.reshape(1, -1))
    w_in_b = w_in.astype(BF16)
    w_out_b = w_out.astype(BF16)

    for layer in range(DEPTH):
        j = layer // 2
        g = ln_g[layer].reshape(1, -1)
        b = ln_b[layer].reshape(1, -1)
        args = (x, w_in_b, w_out_b, g, b, kt, v)
        if layer % 2 == 0:
            extra = [hgrn_lb_logits, hgrn_norm_w[j].reshape(1, -1)]
            state = pltpu.VMEM((HG_HEADS, HG_DK, HG_DK), F32)
            fast = functools.partial(
                _layer_call, functools.partial(_hgrn_layer_kernel, lb_index=j), "hgrn_layer",
                layer=layer, tile=TOKEN_TILE, extra=extra, extra_scratch=[state] + [
                    pltpu.VMEM((SUB_TILE, 4 * GROUP), F32) for _ in range(HG_HEADS)])
            safe = functools.partial(
                _layer_call, functools.partial(_hgrn_safe_layer_kernel, lb_index=j), "hgrn_safe_layer",
                layer=layer, tile=SUB_TILE, extra=extra,
                extra_scratch=[state, pltpu.VMEM((SUB_TILE, 4 * D_MIX), F32)])
            log_lb = jax.nn.log_softmax(hgrn_lb_logits.astype(F32), axis=0)
            log_lb = jax.nn.logsumexp(log_lb[:j + 1], axis=0)
            fast_ok = 0.5 * HG_CHUNK * jnp.max(-log_lb) <= MAX_EXPONENT
            x = lax.cond(fast_ok, fast, safe, *args)
        else:
            x = _layer_call(_conv_layer_kernel, "conv_layer", *args, layer=layer, tile=TOKEN_TILE,
                            extra=[conv_w[j]], extra_scratch=[pltpu.VMEM((TOKEN_TILE + 8, D_MIX), F32)])
    return x
```

```python
import functools
import math

import jax
import jax.numpy as jnp
from jax import lax
from jax.experimental import pallas as pl
from jax.experimental.pallas import tpu as pltpu

D_MODEL = 1024
N_MEM = 256
HG_HEADS = 8
HG_DK = 128
D_MIX = 1024
CONV_WIDTH = 3
XA_HEADS = 4
XA_DH = 128
D_XA = XA_HEADS * XA_DH
D_IN = 4 * D_MIX + D_XA
D_CAT = D_MIX + D_XA
DEPTH = 2
DN_ALPHA = (2 * DEPTH) ** 0.25
LN_EPS = 1e-5
RMS_EPS = 1e-5

SUB_TILE = 512
HGRN_TILE = 2 * SUB_TILE
CONV_TILE = SUB_TILE
PROJ_LEAD = 8
MEM_BATCH = 4
HG_CHUNK = 128
GROUP = HG_DK
N_GROUPS = D_MIX // GROUP
OUT_ROW_SPLIT = 2
SAFE_BLOCK = 8
MAX_EXPONENT = 80.0
VMEM_LIMIT_BYTES = 58 * 1024 * 1024

F32 = jnp.float32
BF16 = jnp.bfloat16
LOG2E = math.log2(math.e)


def _layer_norm_rows(v, g, b):
    mu = jnp.mean(v, axis=-1, keepdims=True)
    d = v - mu
    var = jnp.mean(d * d, axis=-1, keepdims=True)
    return d * lax.rsqrt(var + LN_EPS) * g + b


def _sigmoid(v):
    return 1.0 / (1.0 + jnp.exp(-v))


def _dot(a, b):
    return jnp.dot(a, b, preferred_element_type=F32)


def _dot_nt(a, b):
    return lax.dot_general(a, b, (((1,), (1,)), ((), ())), preferred_element_type=F32)


def _dot_tn(a, b):
    return lax.dot_general(a, b, (((0,), (0,)), ((), ())), preferred_element_type=F32)


def _mem_kv_kernel(mem_ref, w_ref, g_ref, b_ref, kt_ref, v_ref):
    m = _layer_norm_rows(mem_ref[...].reshape(MEM_BATCH * N_MEM, D_MODEL), g_ref[...], b_ref[...])
    kv = _dot(m.astype(BF16), w_ref[...])
    for i in range(MEM_BATCH):
        rows = slice(i * N_MEM, (i + 1) * N_MEM)
        kt_ref[i] = kv[rows, :D_XA].T.astype(BF16)
        v_ref[i] = kv[rows, D_XA:].astype(BF16)


def _mem_kv(mem, w_mem_kv, g, b):
    bsz = mem.shape[0]
    return pl.pallas_call(
        _mem_kv_kernel,
        grid=(bsz // MEM_BATCH,),
        in_specs=[
            pl.BlockSpec((MEM_BATCH, N_MEM, D_MODEL), lambda i: (i, 0, 0)),
            pl.BlockSpec((D_MODEL, 2 * D_XA), lambda i: (0, 0)),
            pl.BlockSpec((1, D_MODEL), lambda i: (0, 0)),
            pl.BlockSpec((1, D_MODEL), lambda i: (0, 0)),
        ],
        out_specs=[
            pl.BlockSpec((MEM_BATCH, D_XA, N_MEM), lambda i: (i, 0, 0)),
            pl.BlockSpec((MEM_BATCH, N_MEM, D_XA), lambda i: (i, 0, 0)),
        ],
        out_shape=[
            jax.ShapeDtypeStruct((bsz, D_XA, N_MEM), BF16),
            jax.ShapeDtypeStruct((bsz, N_MEM, D_XA), BF16),
        ],
        compiler_params=pltpu.CompilerParams(
            dimension_semantics=("arbitrary",), vmem_limit_bytes=VMEM_LIMIT_BYTES),
        name="mem_kv",
    )(mem, w_mem_kv, g, b)


def _sub_rows(r0):
    return slice(r0, r0 + SUB_TILE)


def _mix_proj(xb, w_in_ref, g, parts):
    w = [w_in_ref[:, p * D_MIX + g * GROUP:p * D_MIX + (g + 1) * GROUP] for p in parts]
    return _dot(xb, jnp.concatenate(w, axis=1))


def _forget_lower_bound(lbl_ref, lb_index):
    lg = lbl_ref[...]
    lg = jnp.exp(lg - jnp.max(lg, axis=0, keepdims=True))
    return jnp.sum(lg[:lb_index + 1], axis=0, keepdims=True) / jnp.sum(lg, axis=0, keepdims=True)


def _cross_attn_tasks(xb, r0, w_in_ref, kt_ref, v_ref, cat_ref, zx_ref):
    c = XA_DH ** -0.5 * LOG2E
    rows = _sub_rows(r0)
    scores = {}

    def queries(n):
        cols = slice(n * 2 * XA_DH, (n + 1) * 2 * XA_DH)
        zx_ref[rows, cols] = _dot(
            xb, w_in_ref[:, 4 * D_MIX + cols.start:4 * D_MIX + cols.stop]).astype(BF16)

    def logits(h):
        hs = slice(h * XA_DH, (h + 1) * XA_DH)
        scores[h] = _dot(zx_ref[rows, hs], kt_ref[0, hs, :])

    def read(h):
        hs = slice(h * XA_DH, (h + 1) * XA_DH)
        s = scores.pop(h)
        e = jnp.exp2((s - jnp.max(s, axis=-1, keepdims=True)) * c)
        l = jnp.sum(e, axis=-1, keepdims=True)
        o = _dot(e.astype(BF16), v_ref[0, :, hs]) * (1.0 / l)
        cat_ref[rows, D_MIX + h * XA_DH:D_MIX + (h + 1) * XA_DH] = o.astype(BF16)

    tasks = [functools.partial(queries, 0), functools.partial(queries, 1)]
    for h in range(XA_HEADS):
        tasks += [functools.partial(logits, h), functools.partial(read, h)]
    return tasks


def _out_proj_norm(r0, x_ref, cat_ref, w_out_ref, g_ref, b_ref, o_ref):
    t = SUB_TILE // OUT_ROW_SPLIT
    slabs = [slice(r0 + r * t, r0 + (r + 1) * t) for r in range(OUT_ROW_SPLIT)]
    ys = [_dot(cat_ref[rows, :], w_out_ref[...]) for rows in slabs]
    for rows, y in zip(slabs, ys):
        o_ref[0, rows, :] = _layer_norm_rows(DN_ALPHA * x_ref[0, rows, :] + y, g_ref[...], b_ref[...])


def _hgrn_layer_kernel(x_ref, w_in_ref, w_out_ref, g_ref, b_ref, kt_ref, v_ref,
                       lbl_ref, nw_ref, o_ref, cat_ref, zx_ref, st_ref, *z_refs, lb_index):
    @pl.when(pl.program_id(1) == 0)
    def _():
        st_ref[...] = jnp.zeros_like(st_ref)

    n_sub = x_ref.shape[1] // SUB_TILE
    xbs = [x_ref[0, _sub_rows(k * SUB_TILE), :].astype(BF16) for k in range(n_sub)]
    lb_all = _forget_lower_bound(lbl_ref, lb_index)
    nw = nw_ref[...]

    c_len = HG_CHUNK
    n_chunks = SUB_TILE // c_len
    sub_items = HG_HEADS * n_chunks
    n_items = n_sub * sub_items
    row = lax.broadcasted_iota(jnp.int32, (c_len, c_len), 0)
    col = lax.broadcasted_iota(jnp.int32, (c_len, c_len), 1)
    causal = col <= row
    tri = causal.astype(BF16)
    half = 2 * GROUP

    def item(i):
        k, rem = divmod(i, sub_items)
        h, ci = divmod(rem, n_chunks)
        zrows = slice(ci * c_len, (ci + 1) * c_len)
        return h, ci, zrows, slice(k * SUB_TILE + zrows.start, k * SUB_TILE + zrows.stop)

    def proj_piece(q):
        k, rem = divmod(q, 2 * HG_HEADS)
        h, n = divmod(rem, 2)
        z_refs[h][:, n * half:(n + 1) * half] = _mix_proj(xbs[k], w_in_ref, h, (2 * n, 2 * n + 1))

    def stage_a(i):
        h, _, zrows, _ = item(i)
        lb = lb_all[:, h * HG_DK:(h + 1) * HG_DK]
        zf = z_refs[h][zrows, HG_DK:2 * HG_DK]
        f = lb + (1.0 - lb) * _sigmoid(zf)
        logf = jnp.log(f)
        hi = logf.astype(BF16)
        lo = (logf - hi.astype(F32)).astype(BF16)
        bb = _dot(tri, jnp.concatenate([hi, lo], axis=1))
        return dict(kk=1.0 - f, bb=bb)

    def stage_b(i, v):
        h, _, zrows, _ = item(i)
        zq = z_refs[h][zrows, 0:HG_DK]
        zi = z_refs[h][zrows, 2 * HG_DK:3 * HG_DK]
        bc = v["bb"][:, :HG_DK] + v["bb"][:, HG_DK:]
        b_last = bc[c_len - 1:c_len, :]
        mid = 0.5 * b_last
        e_mid = jnp.exp(mid)
        qh = zq * _sigmoid(zq) * jnp.exp(bc - mid)
        kh = v["kk"] * jnp.exp(mid - bc)
        q_in = (qh * e_mid).astype(BF16)
        k_dec = (kh * e_mid).astype(BF16)
        vt = zi.T.astype(BF16)
        return dict(q_in=q_in, vt=vt, decay=jnp.exp(b_last),
                    scores=_dot_nt(qh.astype(BF16), kh.astype(BF16)), kv=_dot(vt, k_dec))

    def stage_c(v, st):
        a = jnp.where(causal, v["scores"], 0.0).astype(BF16)
        o = _dot_nt(jnp.concatenate([v["q_in"], a], axis=1),
                    jnp.concatenate([st.astype(BF16), v["vt"]], axis=1))
        return o, st * v["decay"] + v["kv"]

    def stage_d(i, o):
        h, _, zrows, rows = item(i)
        zg = z_refs[h][zrows, 3 * HG_DK:4 * HG_DK]
        o = o * lax.rsqrt(jnp.mean(o * o, axis=-1, keepdims=True) + RMS_EPS) * nw
        cat_ref[rows, h * HG_DK:(h + 1) * HG_DK] = (o * (zg * _sigmoid(zg))).astype(BF16)

    n_pieces = n_sub * 2 * HG_HEADS
    attn_tasks = [_cross_attn_tasks(xbs[k], k * SUB_TILE, w_in_ref, kt_ref, v_ref, cat_ref, zx_ref)
                  for k in range(n_sub)]
    emitted = 0
    while 2 * emitted - PROJ_LEAD < -3:
        proj_piece(emitted)
        emitted += 1
    va, vb, vo, st = {}, {}, {}, None
    for s in range(-3, n_items):
        if emitted < n_pieces and 2 * emitted - PROJ_LEAD <= s:
            proj_piece(emitted)
            emitted += 1
        elif s >= 0 and attn_tasks[s // sub_items]:
            attn_tasks[s // sub_items].pop(0)()
        if 0 <= s + 3 < n_items:
            va[s + 3] = stage_a(s + 3)
        if 0 <= s + 2 < n_items:
            vb[s + 2] = stage_b(s + 2, va.pop(s + 2))
        if 0 <= s + 1 < n_items:
            h, ci, _, _ = item(s + 1)
            if ci == 0:
                st = st_ref[h]
            vo[s + 1], st = stage_c(vb.pop(s + 1), st)
            if ci == n_chunks - 1:
                st_ref[h] = st
        if 0 <= s:
            stage_d(s, vo.pop(s))
            k, rem = divmod(s + 1, sub_items)
            if rem == 0:
                while attn_tasks[k - 1]:
                    attn_tasks[k - 1].pop(0)()
                _out_proj_norm((k - 1) * SUB_TILE, x_ref, cat_ref, w_out_ref, g_ref, b_ref, o_ref)


def _hgrn_safe_layer_kernel(x_ref, w_in_ref, w_out_ref, g_ref, b_ref, kt_ref, v_ref,
                            lbl_ref, nw_ref, o_ref, cat_ref, zx_ref, st_ref, z_ref, *, lb_index):
    @pl.when(pl.program_id(1) == 0)
    def _():
        st_ref[...] = jnp.zeros_like(st_ref)

    xb = x_ref[0].astype(BF16)
    for p in range(4):
        for g in range(0, N_GROUPS, 2):
            cols = slice(p * D_MIX + g * GROUP, p * D_MIX + (g + 2) * GROUP)
            z_ref[:, cols] = _dot(xb, w_in_ref[:, cols])
    for task in _cross_attn_tasks(xb, 0, w_in_ref, kt_ref, v_ref, cat_ref, zx_ref):
        task()

    lb_all = _forget_lower_bound(lbl_ref, lb_index)
    nw = nw_ref[...]
    n = SAFE_BLOCK
    row = lax.broadcasted_iota(jnp.int32, (n, HG_DK), 0)
    shifts = [1 << j for j in range(n.bit_length() - 1)]

    def block(rows, h):
        hs = slice(h * HG_DK, (h + 1) * HG_DK)
        zq = z_ref[rows, h * HG_DK:(h + 1) * HG_DK]
        zf = z_ref[rows, D_MIX + h * HG_DK:D_MIX + (h + 1) * HG_DK]
        v = z_ref[rows, 2 * D_MIX + h * HG_DK:2 * D_MIX + (h + 1) * HG_DK]
        zg = z_ref[rows, 3 * D_MIX + h * HG_DK:3 * D_MIX + (h + 1) * HG_DK]
        lb = lb_all[:, hs]
        q = zq * _sigmoid(zq)
        f = lb + (1.0 - lb) * _sigmoid(zf)
        k = 1.0 - f
        pre = f
        suf = jnp.where(row < n - 1, pltpu.roll(f, n - 1, 0), 1.0)
        for sh in shifts:
            pre = pre * jnp.where(row >= sh, pltpu.roll(pre, sh, 0), 1.0)
            suf = suf * jnp.where(row + sh < n, pltpu.roll(suf, n - sh, 0), 1.0)
        st = st_ref[h]
        o = _dot_nt(q * pre, st)
        dec = None
        for lag in range(n):
            dec = jnp.ones_like(f) if lag == 0 else dec * pltpu.roll(f, lag - 1, 0)
            k_s = k if lag == 0 else pltpu.roll(k, lag, 0)
            v_s = v if lag == 0 else pltpu.roll(v, lag, 0)
            a = jnp.sum(q * k_s * dec, axis=-1, keepdims=True)
            o = o + jnp.where(row >= lag, a * v_s, 0.0)
        st_ref[h] = st * pre[n - 1:n, :] + _dot_tn(v, k * suf)
        o = o * lax.rsqrt(jnp.mean(o * o, axis=-1, keepdims=True) + RMS_EPS) * nw
        return o * (zg * _sigmoid(zg))

    def block_pair(bi, carry):
        r0 = pl.multiple_of(bi * 2 * n, 2 * n)
        for h in range(HG_HEADS):
            o = [block(pl.ds(r0 + j * n, n), h) for j in range(2)]
            cat_ref[pl.ds(r0, 2 * n), h * HG_DK:(h + 1) * HG_DK] = jnp.concatenate(o, axis=0).astype(BF16)
        return carry

    lax.fori_loop(0, SUB_TILE // (2 * n), block_pair, 0)
    _out_proj_norm(0, x_ref, cat_ref, w_out_ref, g_ref, b_ref, o_ref)


def _conv_layer_kernel(x_ref, w_in_ref, w_out_ref, g_ref, b_ref, kt_ref, v_ref,
                       cw_ref, o_ref, cat_ref, zx_ref, u_ref):
    @pl.when(pl.program_id(1) == 0)
    def _():
        u_ref[0:8, :] = jnp.zeros((8, D_MIX), F32)

    tile = x_ref.shape[1]
    n_sub = tile // SUB_TILE
    xbs = [x_ref[0, _sub_rows(k * SUB_TILE), :].astype(BF16) for k in range(n_sub)]
    cw = cw_ref[...]

    def mix(k, g, z):
        gs = slice(g * GROUP, (g + 1) * GROUP)
        r0 = 8 + k * SUB_TILE
        u = z[:, GROUP:2 * GROUP] * z[:, 2 * GROUP:3 * GROUP]
        u_ref[r0:r0 + SUB_TILE, gs] = u
        conv = (cw[0:1, gs] * u_ref[r0 - 2:r0 - 2 + SUB_TILE, gs]
                + cw[1:2, gs] * u_ref[r0 - 1:r0 - 1 + SUB_TILE, gs] + cw[2:3, gs] * u)
        if k == n_sub - 1:
            u_ref[0:8, gs] = u_ref[tile:tile + 8, gs]
        zg = z[:, 3 * GROUP:4 * GROUP]
        cat_ref[_sub_rows(k * SUB_TILE), gs] = (z[:, 0:GROUP] * conv * (zg * _sigmoid(zg))).astype(BF16)

    attn_tasks = [_cross_attn_tasks(xbs[k], k * SUB_TILE, w_in_ref, kt_ref, v_ref, cat_ref, zx_ref)
                  for k in range(n_sub)]
    n_steps = n_sub * N_GROUPS
    z_next = _mix_proj(xbs[0], w_in_ref, 0, (0, 1, 2, 3))
    for s in range(n_steps):
        k, g = divmod(s, N_GROUPS)
        z = z_next
        if s + 1 < n_steps:
            k1, g1 = divmod(s + 1, N_GROUPS)
            z_next = _mix_proj(xbs[k1], w_in_ref, g1, (0, 1, 2, 3))
        for _ in range(-(-len(attn_tasks[k]) // (N_GROUPS - g))):
            attn_tasks[k].pop(0)()
        mix(k, g, z)
        if g == N_GROUPS - 1:
            _out_proj_norm(k * SUB_TILE, x_ref, cat_ref, w_out_ref, g_ref, b_ref, o_ref)


def _layer_call(body, name, x, w_in, w_out, g, b, kt, v, *, layer, tile, extra, extra_scratch):
    bsz, slen, _ = x.shape
    const2 = lambda i, j: (0, 0)
    weights = lambda i, j: (layer, 0, 0)
    single = pl.Buffered(1)
    in_specs = [
        pl.BlockSpec((1, tile, D_MODEL), lambda i, j: (i, j, 0)),
        pl.BlockSpec((None, D_MODEL, D_IN), weights, pipeline_mode=single),
        pl.BlockSpec((None, D_CAT, D_MODEL), weights, pipeline_mode=single),
        pl.BlockSpec((1, D_MODEL), const2),
        pl.BlockSpec((1, D_MODEL), const2),
        pl.BlockSpec((1, D_XA, N_MEM), lambda i, j: (i, 0, 0)),
        pl.BlockSpec((1, N_MEM, D_XA), lambda i, j: (i, 0, 0)),
    ] + [pl.BlockSpec(e.shape, const2) for e in extra]
    return pl.pallas_call(
        body,
        grid=(bsz, slen // tile),
        in_specs=in_specs,
        out_specs=pl.BlockSpec((1, tile, D_MODEL), lambda i, j: (i, j, 0)),
        out_shape=jax.ShapeDtypeStruct((bsz, slen, D_MODEL), F32),
        scratch_shapes=[pltpu.VMEM((tile, D_CAT), BF16), pltpu.VMEM((tile, D_XA), BF16)] + extra_scratch,
        compiler_params=pltpu.CompilerParams(
            dimension_semantics=("arbitrary", "arbitrary"),
            vmem_limit_bytes=VMEM_LIMIT_BYTES),
        name=name,
    )(x, w_in, w_out, g, b, kt, v, *extra)


def kernel(x, mem, w_in, w_out, ln_g, ln_b, hgrn_lb_logits, hgrn_norm_w, conv_w,
           mem_ln_g, mem_ln_b, w_mem_kv):
    bsz, slen, d_model = x.shape
    assert d_model == D_MODEL and slen % HGRN_TILE == 0 and slen % CONV_TILE == 0 and bsz % MEM_BATCH == 0
    assert w_in.shape == (DEPTH, D_MODEL, D_IN) and w_out.shape == (DEPTH, D_CAT, D_MODEL)
    assert mem.shape == (bsz, N_MEM, D_MODEL)

    kt, v = _mem_kv(mem, w_mem_kv.astype(BF16), mem_ln_g.reshape(1, -1), mem_ln_b.reshape(1, -1))
    w_in_b = w_in.astype(BF16)
    w_out_b = w_out.astype(BF16)

    for layer in range(DEPTH):
        j = layer // 2
        g = ln_g[layer].reshape(1, -1)
        b = ln_b[layer].reshape(1, -1)
        args = (x, w_in_b, w_out_b, g, b, kt, v)
        if layer % 2 == 0:
            extra = [hgrn_lb_logits, hgrn_norm_w[j].reshape(1, -1)]
            state = pltpu.VMEM((HG_HEADS, HG_DK, HG_DK), F32)
            fast = functools.partial(
                _layer_call, functools.partial(_hgrn_layer_kernel, lb_index=j), "hgrn_layer",
                layer=layer, tile=HGRN_TILE, extra=extra, extra_scratch=[state] + [
                    pltpu.VMEM((SUB_TILE, 4 * GROUP), F32) for _ in range(HG_HEADS)])
            safe = functools.partial(
                _layer_call, functools.partial(_hgrn_safe_layer_kernel, lb_index=j), "hgrn_safe_layer",
                layer=layer, tile=SUB_TILE, extra=extra,
                extra_scratch=[state, pltpu.VMEM((SUB_TILE, 4 * D_MIX), F32)])
            log_lb = jax.nn.log_softmax(hgrn_lb_logits.astype(F32), axis=0)
            log_lb = jax.nn.logsumexp(log_lb[:j + 1], axis=0)
            fast_ok = 0.5 * HG_CHUNK * jnp.max(-log_lb) <= MAX_EXPONENT
            x = lax.cond(fast_ok, fast, safe, *args)
        else:
            x = _layer_call(_conv_layer_kernel, "conv_layer", *args, layer=layer, tile=CONV_TILE,
                            extra=[conv_w[j]], extra_scratch=[pltpu.VMEM((CONV_TILE + 8, D_MIX), F32)])
    return x
```

```python
import functools
import math

import jax
import jax.numpy as jnp
from jax import lax
from jax.experimental import pallas as pl
from jax.experimental.pallas import tpu as pltpu

D_MODEL = 1024
N_MEM = 256
HG_HEADS = 8
HG_DK = 128
D_MIX = 1024
CONV_WIDTH = 3
XA_HEADS = 4
XA_DH = 128
D_XA = XA_HEADS * XA_DH
D_IN = 4 * D_MIX + D_XA
D_CAT = D_MIX + D_XA
DEPTH = 2
DN_ALPHA = (2 * DEPTH) ** 0.25
LN_EPS = 1e-5
RMS_EPS = 1e-5

SUB_TILE = 512
HGRN_TILE = 2 * SUB_TILE
CONV_TILE = SUB_TILE
PROJ_LEAD = 8
MEM_BATCH = 4
HG_CHUNK = 128
GROUP = HG_DK
N_GROUPS = D_MIX // GROUP
OUT_ROW_SPLIT = 2
SAFE_BLOCK = 8
MAX_EXPONENT = 80.0
VMEM_LIMIT_BYTES = 58 * 1024 * 1024

F32 = jnp.float32
BF16 = jnp.bfloat16
LOG2E = math.log2(math.e)


def _layer_norm_rows(v, g, b):
    mu = jnp.mean(v, axis=-1, keepdims=True)
    d = v - mu
    var = jnp.mean(d * d, axis=-1, keepdims=True)
    return d * lax.rsqrt(var + LN_EPS) * g + b


def _sigmoid(v):
    return 1.0 / (1.0 + jnp.exp(-v))


def _dot(a, b):
    return jnp.dot(a, b, preferred_element_type=F32)


def _dot_nt(a, b):
    return lax.dot_general(a, b, (((1,), (1,)), ((), ())), preferred_element_type=F32)


def _dot_tn(a, b):
    return lax.dot_general(a, b, (((0,), (0,)), ((), ())), preferred_element_type=F32)


def _mem_kv_kernel(mem_ref, w_ref, g_ref, b_ref, kt_ref, v_ref):
    m = _layer_norm_rows(mem_ref[...].reshape(MEM_BATCH * N_MEM, D_MODEL), g_ref[...], b_ref[...])
    kv = _dot(m.astype(BF16), w_ref[...])
    for i in range(MEM_BATCH):
        rows = slice(i * N_MEM, (i + 1) * N_MEM)
        kt_ref[i] = kv[rows, :D_XA].T.astype(BF16)
        v_ref[i] = kv[rows, D_XA:].astype(BF16)


def _mem_kv(mem, w_mem_kv, g, b):
    bsz = mem.shape[0]
    return pl.pallas_call(
        _mem_kv_kernel,
        grid=(bsz // MEM_BATCH,),
        in_specs=[
            pl.BlockSpec((MEM_BATCH, N_MEM, D_MODEL), lambda i: (i, 0, 0)),
            pl.BlockSpec((D_MODEL, 2 * D_XA), lambda i: (0, 0)),
            pl.BlockSpec((1, D_MODEL), lambda i: (0, 0)),
            pl.BlockSpec((1, D_MODEL), lambda i: (0, 0)),
        ],
        out_specs=[
            pl.BlockSpec((MEM_BATCH, D_XA, N_MEM), lambda i: (i, 0, 0)),
            pl.BlockSpec((MEM_BATCH, N_MEM, D_XA), lambda i: (i, 0, 0)),
        ],
        out_shape=[
            jax.ShapeDtypeStruct((bsz, D_XA, N_MEM), BF16),
            jax.ShapeDtypeStruct((bsz, N_MEM, D_XA), BF16),
        ],
        compiler_params=pltpu.CompilerParams(
            dimension_semantics=("arbitrary",), vmem_limit_bytes=VMEM_LIMIT_BYTES),
        name="mem_kv",
    )(mem, w_mem_kv, g, b)


def _sub_rows(r0):
    return slice(r0, r0 + SUB_TILE)


def _mix_proj(xb, w_in_ref, g, parts):
    w = [w_in_ref[:, p * D_MIX + g * GROUP:p * D_MIX + (g + 1) * GROUP] for p in parts]
    return _dot(xb, jnp.concatenate(w, axis=1))


def _cumsum_rows(x):
    sub = 8
    row = lax.broadcasted_iota(jnp.int32, (sub, x.shape[1]), 0)
    out, carry = [], None
    for i in range(x.shape[0] // sub):
        blk = x[i * sub:(i + 1) * sub]
        for sh in (1, 2, 4):
            blk = blk + jnp.where(row >= sh, pltpu.roll(blk, sh, 0), 0.0)
        if carry is not None:
            blk = blk + carry
        carry = blk[sub - 1:sub, :]
        out.append(blk)
    return jnp.concatenate(out, axis=0)


def _forget_lower_bound(lbl_ref, lb_index):
    lg = lbl_ref[...]
    lg = jnp.exp(lg - jnp.max(lg, axis=0, keepdims=True))
    return jnp.sum(lg[:lb_index + 1], axis=0, keepdims=True) / jnp.sum(lg, axis=0, keepdims=True)


def _cross_attn_tasks(xb, r0, w_in_ref, kt_ref, v_ref, cat_ref, zx_ref):
    c = XA_DH ** -0.5 * LOG2E
    rows = _sub_rows(r0)
    scores = {}

    def queries(n):
        cols = slice(n * 2 * XA_DH, (n + 1) * 2 * XA_DH)
        zx_ref[rows, cols] = _dot(
            xb, w_in_ref[:, 4 * D_MIX + cols.start:4 * D_MIX + cols.stop]).astype(BF16)

    def logits(h):
        hs = slice(h * XA_DH, (h + 1) * XA_DH)
        scores[h] = _dot(zx_ref[rows, hs], kt_ref[0, hs, :])

    def read(h):
        hs = slice(h * XA_DH, (h + 1) * XA_DH)
        s = scores.pop(h)
        e = jnp.exp2((s - jnp.max(s, axis=-1, keepdims=True)) * c)
        l = jnp.sum(e, axis=-1, keepdims=True)
        o = _dot(e.astype(BF16), v_ref[0, :, hs]) * (1.0 / l)
        cat_ref[rows, D_MIX + h * XA_DH:D_MIX + (h + 1) * XA_DH] = o.astype(BF16)

    tasks = [functools.partial(queries, 0), functools.partial(queries, 1)]
    for h in range(XA_HEADS):
        tasks += [functools.partial(logits, h), functools.partial(read, h)]
    return tasks


def _out_proj_norm(r0, x_ref, cat_ref, w_out_ref, g_ref, b_ref, o_ref):
    t = SUB_TILE // OUT_ROW_SPLIT
    slabs = [slice(r0 + r * t, r0 + (r + 1) * t) for r in range(OUT_ROW_SPLIT)]
    ys = [_dot(cat_ref[rows, :], w_out_ref[...]) for rows in slabs]
    for rows, y in zip(slabs, ys):
        o_ref[0, rows, :] = _layer_norm_rows(DN_ALPHA * x_ref[0, rows, :] + y, g_ref[...], b_ref[...])


def _hgrn_layer_kernel(x_ref, w_in_ref, w_out_ref, g_ref, b_ref, kt_ref, v_ref,
                       lbl_ref, nw_ref, o_ref, cat_ref, zx_ref, st_ref, *z_refs, lb_index):
    @pl.when(pl.program_id(1) == 0)
    def _():
        st_ref[...] = jnp.zeros_like(st_ref)

    n_sub = x_ref.shape[1] // SUB_TILE
    xbs = [x_ref[0, _sub_rows(k * SUB_TILE), :].astype(BF16) for k in range(n_sub)]
    lb_all = _forget_lower_bound(lbl_ref, lb_index)
    nw = nw_ref[...]

    c_len = HG_CHUNK
    n_chunks = SUB_TILE // c_len
    sub_items = HG_HEADS * n_chunks
    n_items = n_sub * sub_items
    row = lax.broadcasted_iota(jnp.int32, (c_len, c_len), 0)
    col = lax.broadcasted_iota(jnp.int32, (c_len, c_len), 1)
    causal = col <= row
    half = 2 * GROUP

    def item(i):
        k, rem = divmod(i, sub_items)
        h, ci = divmod(rem, n_chunks)
        zrows = slice(ci * c_len, (ci + 1) * c_len)
        return h, ci, zrows, slice(k * SUB_TILE + zrows.start, k * SUB_TILE + zrows.stop)

    def proj_piece(q):
        k, rem = divmod(q, 2 * HG_HEADS)
        h, n = divmod(rem, 2)
        z_refs[h][:, n * half:(n + 1) * half] = _mix_proj(xbs[k], w_in_ref, h, (2 * n, 2 * n + 1))

    def stage_a(i):
        h, _, zrows, _ = item(i)
        lb = lb_all[:, h * HG_DK:(h + 1) * HG_DK]
        zf = z_refs[h][zrows, HG_DK:2 * HG_DK]
        f = lb + (1.0 - lb) * _sigmoid(zf)
        return dict(kk=1.0 - f, bc=_cumsum_rows(jnp.log(f)))

    def stage_b(i, v):
        h, _, zrows, _ = item(i)
        zq = z_refs[h][zrows, 0:HG_DK]
        zi = z_refs[h][zrows, 2 * HG_DK:3 * HG_DK]
        bc = v["bc"]
        b_last = bc[c_len - 1:c_len, :]
        mid = 0.5 * b_last
        e_mid = jnp.exp(mid)
        qh = zq * _sigmoid(zq) * jnp.exp(bc - mid)
        kh = v["kk"] * jnp.exp(mid - bc)
        q_in = (qh * e_mid).astype(BF16)
        k_dec = (kh * e_mid).astype(BF16)
        vt = zi.T.astype(BF16)
        return dict(q_in=q_in, vt=vt, decay=jnp.exp(b_last),
                    scores=_dot_nt(qh.astype(BF16), kh.astype(BF16)), kv=_dot(vt, k_dec))

    def stage_c(v, st):
        a = jnp.where(causal, v["scores"], 0.0).astype(BF16)
        o = _dot_nt(jnp.concatenate([v["q_in"], a], axis=1),
                    jnp.concatenate([st.astype(BF16), v["vt"]], axis=1))
        return o, st * v["decay"] + v["kv"]

    def stage_d(i, o):
        h, _, zrows, rows = item(i)
        zg = z_refs[h][zrows, 3 * HG_DK:4 * HG_DK]
        o = o * lax.rsqrt(jnp.mean(o * o, axis=-1, keepdims=True) + RMS_EPS) * nw
        cat_ref[rows, h * HG_DK:(h + 1) * HG_DK] = (o * (zg * _sigmoid(zg))).astype(BF16)

    n_pieces = n_sub * 2 * HG_HEADS
    attn_tasks = [_cross_attn_tasks(xbs[k], k * SUB_TILE, w_in_ref, kt_ref, v_ref, cat_ref, zx_ref)
                  for k in range(n_sub)]
    emitted = 0
    while 2 * emitted - PROJ_LEAD < -3:
        proj_piece(emitted)
        emitted += 1
    va, vb, vo, st = {}, {}, {}, None
    for s in range(-3, n_items):
        if emitted < n_pieces and 2 * emitted - PROJ_LEAD <= s:
            proj_piece(emitted)
            emitted += 1
        elif s >= 0 and attn_tasks[s // sub_items]:
            attn_tasks[s // sub_items].pop(0)()
        if 0 <= s + 3 < n_items:
            va[s + 3] = stage_a(s + 3)
        if 0 <= s + 2 < n_items:
            vb[s + 2] = stage_b(s + 2, va.pop(s + 2))
        if 0 <= s + 1 < n_items:
            h, ci, _, _ = item(s + 1)
            if ci == 0:
                st = st_ref[h]
            vo[s + 1], st = stage_c(vb.pop(s + 1), st)
            if ci == n_chunks - 1:
                st_ref[h] = st
        if 0 <= s:
            stage_d(s, vo.pop(s))
            k, rem = divmod(s + 1, sub_items)
            if rem == 0:
                while attn_tasks[k - 1]:
                    attn_tasks[k - 1].pop(0)()
                _out_proj_norm((k - 1) * SUB_TILE, x_ref, cat_ref, w_out_ref, g_ref, b_ref, o_ref)


def _hgrn_safe_layer_kernel(x_ref, w_in_ref, w_out_ref, g_ref, b_ref, kt_ref, v_ref,
                            lbl_ref, nw_ref, o_ref, cat_ref, zx_ref, st_ref, z_ref, *, lb_index):
    @pl.when(pl.program_id(1) == 0)
    def _():
        st_ref[...] = jnp.zeros_like(st_ref)

    xb = x_ref[0].astype(BF16)
    for p in range(4):
        for g in range(0, N_GROUPS, 2):
            cols = slice(p * D_MIX + g * GROUP, p * D_MIX + (g + 2) * GROUP)
            z_ref[:, cols] = _dot(xb, w_in_ref[:, cols])
    for task in _cross_attn_tasks(xb, 0, w_in_ref, kt_ref, v_ref, cat_ref, zx_ref):
        task()

    lb_all = _forget_lower_bound(lbl_ref, lb_index)
    nw = nw_ref[...]
    n = SAFE_BLOCK
    row = lax.broadcasted_iota(jnp.int32, (n, HG_DK), 0)
    shifts = [1 << j for j in range(n.bit_length() - 1)]

    def block(rows, h):
        hs = slice(h * HG_DK, (h + 1) * HG_DK)
        zq = z_ref[rows, h * HG_DK:(h + 1) * HG_DK]
        zf = z_ref[rows, D_MIX + h * HG_DK:D_MIX + (h + 1) * HG_DK]
        v = z_ref[rows, 2 * D_MIX + h * HG_DK:2 * D_MIX + (h + 1) * HG_DK]
        zg = z_ref[rows, 3 * D_MIX + h * HG_DK:3 * D_MIX + (h + 1) * HG_DK]
        lb = lb_all[:, hs]
        q = zq * _sigmoid(zq)
        f = lb + (1.0 - lb) * _sigmoid(zf)
        k = 1.0 - f
        pre = f
        suf = jnp.where(row < n - 1, pltpu.roll(f, n - 1, 0), 1.0)
        for sh in shifts:
            pre = pre * jnp.where(row >= sh, pltpu.roll(pre, sh, 0), 1.0)
            suf = suf * jnp.where(row + sh < n, pltpu.roll(suf, n - sh, 0), 1.0)
        st = st_ref[h]
        o = _dot_nt(q * pre, st)
        dec = None
        for lag in range(n):
            dec = jnp.ones_like(f) if lag == 0 else dec * pltpu.roll(f, lag - 1, 0)
            k_s = k if lag == 0 else pltpu.roll(k, lag, 0)
            v_s = v if lag == 0 else pltpu.roll(v, lag, 0)
            a = jnp.sum(q * k_s * dec, axis=-1, keepdims=True)
            o = o + jnp.where(row >= lag, a * v_s, 0.0)
        st_ref[h] = st * pre[n - 1:n, :] + _dot_tn(v, k * suf)
        o = o * lax.rsqrt(jnp.mean(o * o, axis=-1, keepdims=True) + RMS_EPS) * nw
        return o * (zg * _sigmoid(zg))

    def block_pair(bi, carry):
        r0 = pl.multiple_of(bi * 2 * n, 2 * n)
        for h in range(HG_HEADS):
            o = [block(pl.ds(r0 + j * n, n), h) for j in range(2)]
            cat_ref[pl.ds(r0, 2 * n), h * HG_DK:(h + 1) * HG_DK] = jnp.concatenate(o, axis=0).astype(BF16)
        return carry

    lax.fori_loop(0, SUB_TILE // (2 * n), block_pair, 0)
    _out_proj_norm(0, x_ref, cat_ref, w_out_ref, g_ref, b_ref, o_ref)


def _conv_layer_kernel(x_ref, w_in_ref, w_out_ref, g_ref, b_ref, kt_ref, v_ref,
                       cw_ref, o_ref, cat_ref, zx_ref, u_ref):
    @pl.when(pl.program_id(1) == 0)
    def _():
        u_ref[0:8, :] = jnp.zeros((8, D_MIX), F32)

    tile = x_ref.shape[1]
    n_sub = tile // SUB_TILE
    xbs = [x_ref[0, _sub_rows(k * SUB_TILE), :].astype(BF16) for k in range(n_sub)]
    cw = cw_ref[...]

    def mix(k, g, z):
        gs = slice(g * GROUP, (g + 1) * GROUP)
        r0 = 8 + k * SUB_TILE
        u = z[:, GROUP:2 * GROUP] * z[:, 2 * GROUP:3 * GROUP]
        u_ref[r0:r0 + SUB_TILE, gs] = u
        conv = (cw[0:1, gs] * u_ref[r0 - 2:r0 - 2 + SUB_TILE, gs]
                + cw[1:2, gs] * u_ref[r0 - 1:r0 - 1 + SUB_TILE, gs] + cw[2:3, gs] * u)
        if k == n_sub - 1:
            u_ref[0:8, gs] = u_ref[tile:tile + 8, gs]
        zg = z[:, 3 * GROUP:4 * GROUP]
        cat_ref[_sub_rows(k * SUB_TILE), gs] = (z[:, 0:GROUP] * conv * (zg * _sigmoid(zg))).astype(BF16)

    attn_tasks = [_cross_attn_tasks(xbs[k], k * SUB_TILE, w_in_ref, kt_ref, v_ref, cat_ref, zx_ref)
                  for k in range(n_sub)]
    n_steps = n_sub * N_GROUPS
    z_next = _mix_proj(xbs[0], w_in_ref, 0, (0, 1, 2, 3))
    for s in range(n_steps):
        k, g = divmod(s, N_GROUPS)
        z = z_next
        if s + 1 < n_steps:
            k1, g1 = divmod(s + 1, N_GROUPS)
            z_next = _mix_proj(xbs[k1], w_in_ref, g1, (0, 1, 2, 3))
        if attn_tasks[k]:
            attn_tasks[k].pop(0)()
        mix(k, g, z)
        if g == N_GROUPS - 1:
            while attn_tasks[k]:
                attn_tasks[k].pop(0)()
            _out_proj_norm(k * SUB_TILE, x_ref, cat_ref, w_out_ref, g_ref, b_ref, o_ref)


def _layer_call(body, name, x, w_in, w_out, g, b, kt, v, *, layer, tile, extra, extra_scratch):
    bsz, slen, _ = x.shape
    const2 = lambda i, j: (0, 0)
    weights = lambda i, j: (layer, 0, 0)
    single = pl.Buffered(1)
    in_specs = [
        pl.BlockSpec((1, tile, D_MODEL), lambda i, j: (i, j, 0)),
        pl.BlockSpec((None, D_MODEL, D_IN), weights, pipeline_mode=single),
        pl.BlockSpec((None, D_CAT, D_MODEL), weights, pipeline_mode=single),
        pl.BlockSpec((1, D_MODEL), const2),
        pl.BlockSpec((1, D_MODEL), const2),
        pl.BlockSpec((1, D_XA, N_MEM), lambda i, j: (i, 0, 0)),
        pl.BlockSpec((1, N_MEM, D_XA), lambda i, j: (i, 0, 0)),
    ] + [pl.BlockSpec(e.shape, const2) for e in extra]
    return pl.pallas_call(
        body,
        grid=(bsz, slen // tile),
        in_specs=in_specs,
        out_specs=pl.BlockSpec((1, tile, D_MODEL), lambda i, j: (i, j, 0)),
        out_shape=jax.ShapeDtypeStruct((bsz, slen, D_MODEL), F32),
        scratch_shapes=[pltpu.VMEM((tile, D_CAT), BF16), pltpu.VMEM((tile, D_XA), BF16)] + extra_scratch,
        compiler_params=pltpu.CompilerParams(
            dimension_semantics=("arbitrary", "arbitrary"),
            vmem_limit_bytes=VMEM_LIMIT_BYTES),
        name=name,
    )(x, w_in, w_out, g, b, kt, v, *extra)


def kernel(x, mem, w_in, w_out, ln_g, ln_b, hgrn_lb_logits, hgrn_norm_w, conv_w,
           mem_ln_g, mem_ln_b, w_mem_kv):
    bsz, slen, d_model = x.shape
    assert d_model == D_MODEL and slen % HGRN_TILE == 0 and slen % CONV_TILE == 0 and bsz % MEM_BATCH == 0
    assert w_in.shape == (DEPTH, D_MODEL, D_IN) and w_out.shape == (DEPTH, D_CAT, D_MODEL)
    assert mem.shape == (bsz, N_MEM, D_MODEL)

    kt, v = _mem_kv(mem, w_mem_kv.astype(BF16), mem_ln_g.reshape(1, -1), mem_ln_b.reshape(1, -1))
    w_in_b = w_in.astype(BF16)
    w_out_b = w_out.astype(BF16)

    for layer in range(DEPTH):
        j = layer // 2
        g = ln_g[layer].reshape(1, -1)
        b = ln_b[layer].reshape(1, -1)
        args = (x, w_in_b, w_out_b, g, b, kt, v)
        if layer % 2 == 0:
            extra = [hgrn_lb_logits, hgrn_norm_w[j].reshape(1, -1)]
            state = pltpu.VMEM((HG_HEADS, HG_DK, HG_DK), F32)
            fast = functools.partial(
                _layer_call, functools.partial(_hgrn_layer_kernel, lb_index=j), "hgrn_layer",
                layer=layer, tile=HGRN_TILE, extra=extra, extra_scratch=[state] + [
                    pltpu.VMEM((SUB_TILE, 4 * GROUP), F32) for _ in range(HG_HEADS)])
            safe = functools.partial(
                _layer_call, functools.partial(_hgrn_safe_layer_kernel, lb_index=j), "hgrn_safe_layer",
                layer=layer, tile=SUB_TILE, extra=extra,
                extra_scratch=[state, pltpu.VMEM((SUB_TILE, 4 * D_MIX), F32)])
            log_lb = jax.nn.log_softmax(hgrn_lb_logits.astype(F32), axis=0)
            log_lb = jax.nn.logsumexp(log_lb[:j + 1], axis=0)
            fast_ok = 0.5 * HG_CHUNK * jnp.max(-log_lb) <= MAX_EXPONENT
            x = lax.cond(fast_ok, fast, safe, *args)
        else:
            x = _layer_call(_conv_layer_kernel, "conv_layer", *args, layer=layer, tile=CONV_TILE,
                            extra=[conv_w[j]], extra_scratch=[pltpu.VMEM((CONV_TILE + 8, D_MIX), F32)])
    return x
```

```python
import functools
import math

import jax
import jax.numpy as jnp
from jax import lax
from jax.experimental import pallas as pl
from jax.experimental.pallas import tpu as pltpu

D_MODEL = 1024
N_MEM = 256
HG_HEADS = 8
HG_DK = 128
D_MIX = 1024
CONV_WIDTH = 3
XA_HEADS = 4
XA_DH = 128
D_XA = XA_HEADS * XA_DH
D_IN = 4 * D_MIX + D_XA
D_CAT = D_MIX + D_XA
DEPTH = 2
DN_ALPHA = (2 * DEPTH) ** 0.25
LN_EPS = 1e-5
RMS_EPS = 1e-5

SUB_TILE = 512
HGRN_TILE = 2 * SUB_TILE
CONV_TILE = SUB_TILE
PROJ_LEAD = 8
MEM_BATCH = 4
HG_CHUNK = 128
GROUP = HG_DK
N_GROUPS = D_MIX // GROUP
OUT_ROW_SPLIT = 2
SUBLANES = 8
SAFE_BLOCK = SUBLANES
MAX_EXPONENT = 80.0
VMEM_LIMIT_BYTES = 58 * 1024 * 1024

F32 = jnp.float32
BF16 = jnp.bfloat16
LOG2E = math.log2(math.e)


def _layer_norm_rows(v, g, b):
    mu = jnp.mean(v, axis=-1, keepdims=True)
    d = v - mu
    var = jnp.mean(d * d, axis=-1, keepdims=True)
    return d * lax.rsqrt(var + LN_EPS) * g + b


def _sigmoid(v):
    return 1.0 / (1.0 + jnp.exp(-v))


def _dot(a, b):
    return jnp.dot(a, b, preferred_element_type=F32)


def _dot_nt(a, b):
    return lax.dot_general(a, b, (((1,), (1,)), ((), ())), preferred_element_type=F32)


def _dot_tn(a, b):
    return lax.dot_general(a, b, (((0,), (0,)), ((), ())), preferred_element_type=F32)


def _mem_kv_kernel(mem_ref, w_ref, g_ref, b_ref, kt_ref, v_ref):
    m = _layer_norm_rows(mem_ref[...].reshape(MEM_BATCH * N_MEM, D_MODEL), g_ref[...], b_ref[...])
    kv = _dot(m.astype(BF16), w_ref[...])
    for i in range(MEM_BATCH):
        rows = slice(i * N_MEM, (i + 1) * N_MEM)
        kt_ref[i] = kv[rows, :D_XA].T.astype(BF16)
        v_ref[i] = kv[rows, D_XA:].astype(BF16)


def _mem_kv(mem, w_mem_kv, g, b):
    bsz = mem.shape[0]
    return pl.pallas_call(
        _mem_kv_kernel,
        grid=(bsz // MEM_BATCH,),
        in_specs=[
            pl.BlockSpec((MEM_BATCH, N_MEM, D_MODEL), lambda i: (i, 0, 0)),
            pl.BlockSpec((D_MODEL, 2 * D_XA), lambda i: (0, 0)),
            pl.BlockSpec((1, D_MODEL), lambda i: (0, 0)),
            pl.BlockSpec((1, D_MODEL), lambda i: (0, 0)),
        ],
        out_specs=[
            pl.BlockSpec((MEM_BATCH, D_XA, N_MEM), lambda i: (i, 0, 0)),
            pl.BlockSpec((MEM_BATCH, N_MEM, D_XA), lambda i: (i, 0, 0)),
        ],
        out_shape=[
            jax.ShapeDtypeStruct((bsz, D_XA, N_MEM), BF16),
            jax.ShapeDtypeStruct((bsz, N_MEM, D_XA), BF16),
        ],
        compiler_params=pltpu.CompilerParams(
            dimension_semantics=("arbitrary",), vmem_limit_bytes=VMEM_LIMIT_BYTES),
        name="mem_kv",
    )(mem, w_mem_kv, g, b)


def _sub_rows(r0):
    return slice(r0, r0 + SUB_TILE)


def _mix_proj(xb, w_in_ref, g, parts):
    w = [w_in_ref[:, p * D_MIX + g * GROUP:p * D_MIX + (g + 1) * GROUP] for p in parts]
    return _dot(xb, jnp.concatenate(w, axis=1))


def _cumsum_rows(x):
    sub = SUBLANES
    row = lax.broadcasted_iota(jnp.int32, (sub, x.shape[1]), 0)
    out, carry = [], None
    for i in range(x.shape[0] // sub):
        blk = x[i * sub:(i + 1) * sub]
        for sh in (1, 2, 4):
            blk = blk + jnp.where(row >= sh, pltpu.roll(blk, sh, 0), 0.0)
        if carry is not None:
            blk = blk + carry
        carry = blk[sub - 1:sub, :]
        out.append(blk)
    return jnp.concatenate(out, axis=0)


def _rms_norm_gate(o, zg, nw):
    dv = o.shape[-1]
    r = lax.rsqrt(jnp.sum(o * o, axis=-1, keepdims=True) + dv * RMS_EPS)
    return o * r * (nw * dv ** 0.5) * (zg * _sigmoid(zg))


def _forget_lower_bound(lbl_ref, lb_index):
    lg = lbl_ref[...]
    lg = jnp.exp(lg - jnp.max(lg, axis=0, keepdims=True))
    return jnp.sum(lg[:lb_index + 1], axis=0, keepdims=True) / jnp.sum(lg, axis=0, keepdims=True)


def _cross_attn_tasks(xb, r0, w_in_ref, kt_ref, v_ref, cat_ref, zx_ref):
    c = XA_DH ** -0.5 * LOG2E
    rows = _sub_rows(r0)
    scores = {}

    def queries(n):
        cols = slice(n * 2 * XA_DH, (n + 1) * 2 * XA_DH)
        zx_ref[rows, cols] = _dot(
            xb, w_in_ref[:, 4 * D_MIX + cols.start:4 * D_MIX + cols.stop]).astype(BF16)

    def logits(h):
        hs = slice(h * XA_DH, (h + 1) * XA_DH)
        scores[h] = _dot(zx_ref[rows, hs], kt_ref[0, hs, :])

    def read(h):
        hs = slice(h * XA_DH, (h + 1) * XA_DH)
        s = scores.pop(h)
        e = jnp.exp2((s - jnp.max(s, axis=-1, keepdims=True)) * c)
        l = jnp.sum(e, axis=-1, keepdims=True)
        o = _dot(e.astype(BF16), v_ref[0, :, hs]) * (1.0 / l)
        cat_ref[rows, D_MIX + h * XA_DH:D_MIX + (h + 1) * XA_DH] = o.astype(BF16)

    tasks = [functools.partial(queries, 0), functools.partial(queries, 1)]
    for h in range(XA_HEADS):
        tasks += [functools.partial(logits, h), functools.partial(read, h)]
    return tasks


def _out_proj_norm(r0, x_ref, cat_ref, w_out_ref, g_ref, b_ref, o_ref):
    t = SUB_TILE // OUT_ROW_SPLIT
    slabs = [slice(r0 + r * t, r0 + (r + 1) * t) for r in range(OUT_ROW_SPLIT)]
    ys = [_dot(cat_ref[rows, :], w_out_ref[...]) for rows in slabs]
    for rows, y in zip(slabs, ys):
        o_ref[0, rows, :] = _layer_norm_rows(DN_ALPHA * x_ref[0, rows, :] + y, g_ref[...], b_ref[...])


def _hgrn_layer_kernel(x_ref, w_in_ref, w_out_ref, g_ref, b_ref, kt_ref, v_ref,
                       lbl_ref, nw_ref, o_ref, cat_ref, zx_ref, st_ref, *z_refs, lb_index):
    @pl.when(pl.program_id(1) == 0)
    def _():
        st_ref[...] = jnp.zeros_like(st_ref)

    n_sub = x_ref.shape[1] // SUB_TILE
    xbs = [x_ref[0, _sub_rows(k * SUB_TILE), :].astype(BF16) for k in range(n_sub)]
    lb_all = _forget_lower_bound(lbl_ref, lb_index)
    nw = nw_ref[...]

    c_len = HG_CHUNK
    n_chunks = SUB_TILE // c_len
    sub_items = HG_HEADS * n_chunks
    n_items = n_sub * sub_items
    row = lax.broadcasted_iota(jnp.int32, (c_len, c_len), 0)
    col = lax.broadcasted_iota(jnp.int32, (c_len, c_len), 1)
    causal = col <= row
    half = 2 * GROUP

    def item(i):
        k, rem = divmod(i, sub_items)
        h, ci = divmod(rem, n_chunks)
        zrows = slice(ci * c_len, (ci + 1) * c_len)
        return h, ci, zrows, slice(k * SUB_TILE + zrows.start, k * SUB_TILE + zrows.stop)

    def proj_piece(q):
        k, rem = divmod(q, 2 * HG_HEADS)
        h, n = divmod(rem, 2)
        z_refs[h][:, n * half:(n + 1) * half] = _mix_proj(xbs[k], w_in_ref, h, (2 * n, 2 * n + 1))

    def stage_a(i):
        h, _, zrows, _ = item(i)
        lb = lb_all[:, h * HG_DK:(h + 1) * HG_DK]
        zf = z_refs[h][zrows, HG_DK:2 * HG_DK]
        f = lb + (1.0 - lb) * _sigmoid(zf)
        return dict(kk=1.0 - f, bc=_cumsum_rows(jnp.log(f)))

    def stage_b(i, v):
        h, _, zrows, _ = item(i)
        zq = z_refs[h][zrows, 0:HG_DK]
        zi = z_refs[h][zrows, 2 * HG_DK:3 * HG_DK]
        bc = v["bc"]
        b_last = bc[c_len - 1:c_len, :]
        mid = 0.5 * b_last
        e_mid = jnp.exp(mid)
        qh = zq * _sigmoid(zq) * jnp.exp(bc - mid)
        kh = v["kk"] * jnp.exp(mid - bc)
        q_in = (qh * e_mid).astype(BF16)
        k_dec = (kh * e_mid).astype(BF16)
        vt = zi.T.astype(BF16)
        return dict(q_in=q_in, vt=vt, decay=jnp.exp(b_last),
                    scores=_dot_nt(qh.astype(BF16), kh.astype(BF16)), kv=_dot(vt, k_dec))

    def stage_c(v, st):
        a = jnp.where(causal, v["scores"], 0.0).astype(BF16)
        o = _dot_nt(jnp.concatenate([v["q_in"], a], axis=1),
                    jnp.concatenate([st.astype(BF16), v["vt"]], axis=1))
        return o, st * v["decay"] + v["kv"]

    def stage_d(i, o):
        h, _, zrows, rows = item(i)
        zg = z_refs[h][zrows, 3 * HG_DK:4 * HG_DK]
        cat_ref[rows, h * HG_DK:(h + 1) * HG_DK] = _rms_norm_gate(o, zg, nw).astype(BF16)

    n_pieces = n_sub * 2 * HG_HEADS
    attn_tasks = [_cross_attn_tasks(xbs[k], k * SUB_TILE, w_in_ref, kt_ref, v_ref, cat_ref, zx_ref)
                  for k in range(n_sub)]
    emitted = 0
    while 2 * emitted - PROJ_LEAD < -3:
        proj_piece(emitted)
        emitted += 1
    va, vb, vo, st = {}, {}, {}, None
    for s in range(-3, n_items):
        if emitted < n_pieces and 2 * emitted - PROJ_LEAD <= s:
            proj_piece(emitted)
            emitted += 1
        elif s >= 0 and attn_tasks[s // sub_items]:
            attn_tasks[s // sub_items].pop(0)()
        if 0 <= s + 3 < n_items:
            va[s + 3] = stage_a(s + 3)
        if 0 <= s + 2 < n_items:
            vb[s + 2] = stage_b(s + 2, va.pop(s + 2))
        if 0 <= s + 1 < n_items:
            h, ci, _, _ = item(s + 1)
            if ci == 0:
                st = st_ref[h]
            vo[s + 1], st = stage_c(vb.pop(s + 1), st)
            if ci == n_chunks - 1:
                st_ref[h] = st
        if 0 <= s:
            stage_d(s, vo.pop(s))
            k, rem = divmod(s + 1, sub_items)
            if rem == 0:
                while attn_tasks[k - 1]:
                    attn_tasks[k - 1].pop(0)()
                _out_proj_norm((k - 1) * SUB_TILE, x_ref, cat_ref, w_out_ref, g_ref, b_ref, o_ref)


def _hgrn_safe_layer_kernel(x_ref, w_in_ref, w_out_ref, g_ref, b_ref, kt_ref, v_ref,
                            lbl_ref, nw_ref, o_ref, cat_ref, zx_ref, st_ref, z_ref, *, lb_index):
    @pl.when(pl.program_id(1) == 0)
    def _():
        st_ref[...] = jnp.zeros_like(st_ref)

    xb = x_ref[0].astype(BF16)
    for p in range(4):
        for g in range(0, N_GROUPS, 2):
            cols = slice(p * D_MIX + g * GROUP, p * D_MIX + (g + 2) * GROUP)
            z_ref[:, cols] = _dot(xb, w_in_ref[:, cols])
    for task in _cross_attn_tasks(xb, 0, w_in_ref, kt_ref, v_ref, cat_ref, zx_ref):
        task()

    lb_all = _forget_lower_bound(lbl_ref, lb_index)
    nw = nw_ref[...]
    n = SAFE_BLOCK
    row = lax.broadcasted_iota(jnp.int32, (n, HG_DK), 0)
    shifts = [1 << j for j in range(n.bit_length() - 1)]

    def block(rows, h):
        hs = slice(h * HG_DK, (h + 1) * HG_DK)
        zq = z_ref[rows, h * HG_DK:(h + 1) * HG_DK]
        zf = z_ref[rows, D_MIX + h * HG_DK:D_MIX + (h + 1) * HG_DK]
        v = z_ref[rows, 2 * D_MIX + h * HG_DK:2 * D_MIX + (h + 1) * HG_DK]
        zg = z_ref[rows, 3 * D_MIX + h * HG_DK:3 * D_MIX + (h + 1) * HG_DK]
        lb = lb_all[:, hs]
        q = zq * _sigmoid(zq)
        f = lb + (1.0 - lb) * _sigmoid(zf)
        k = 1.0 - f
        pre = f
        suf = jnp.where(row < n - 1, pltpu.roll(f, n - 1, 0), 1.0)
        for sh in shifts:
            pre = pre * jnp.where(row >= sh, pltpu.roll(pre, sh, 0), 1.0)
            suf = suf * jnp.where(row + sh < n, pltpu.roll(suf, n - sh, 0), 1.0)
        st = st_ref[h]
        o = _dot_nt(q * pre, st)
        dec = None
        for lag in range(n):
            dec = jnp.ones_like(f) if lag == 0 else dec * pltpu.roll(f, lag - 1, 0)
            k_s = k if lag == 0 else pltpu.roll(k, lag, 0)
            v_s = v if lag == 0 else pltpu.roll(v, lag, 0)
            a = jnp.sum(q * k_s * dec, axis=-1, keepdims=True)
            o = o + jnp.where(row >= lag, a * v_s, 0.0)
        st_ref[h] = st * pre[n - 1:n, :] + _dot_tn(v, k * suf)
        return _rms_norm_gate(o, zg, nw)

    def block_pair(bi, carry):
        r0 = pl.multiple_of(bi * 2 * n, 2 * n)
        for h in range(HG_HEADS):
            o = [block(pl.ds(r0 + j * n, n), h) for j in range(2)]
            cat_ref[pl.ds(r0, 2 * n), h * HG_DK:(h + 1) * HG_DK] = jnp.concatenate(o, axis=0).astype(BF16)
        return carry

    lax.fori_loop(0, SUB_TILE // (2 * n), block_pair, 0)
    _out_proj_norm(0, x_ref, cat_ref, w_out_ref, g_ref, b_ref, o_ref)


def _conv_layer_kernel(x_ref, w_in_ref, w_out_ref, g_ref, b_ref, kt_ref, v_ref,
                       cw_ref, o_ref, cat_ref, zx_ref, u_ref):
    @pl.when(pl.program_id(1) == 0)
    def _():
        u_ref[0:SUBLANES, :] = jnp.zeros((SUBLANES, D_MIX), F32)

    tile = x_ref.shape[1]
    n_sub = tile // SUB_TILE
    xbs = [x_ref[0, _sub_rows(k * SUB_TILE), :].astype(BF16) for k in range(n_sub)]
    cw = cw_ref[...]

    def mix(k, g, z):
        gs = slice(g * GROUP, (g + 1) * GROUP)
        r0 = SUBLANES + k * SUB_TILE
        u = z[:, GROUP:2 * GROUP] * z[:, 2 * GROUP:3 * GROUP]
        u_ref[r0:r0 + SUB_TILE, gs] = u
        conv = cw[CONV_WIDTH - 1:CONV_WIDTH, gs] * u
        for back in range(1, CONV_WIDTH):
            tap = CONV_WIDTH - 1 - back
            conv = conv + cw[tap:tap + 1, gs] * u_ref[r0 - back:r0 - back + SUB_TILE, gs]
        if k == n_sub - 1:
            u_ref[0:SUBLANES, gs] = u_ref[tile:tile + SUBLANES, gs]
        zg = z[:, 3 * GROUP:4 * GROUP]
        cat_ref[_sub_rows(k * SUB_TILE), gs] = (z[:, 0:GROUP] * conv * (zg * _sigmoid(zg))).astype(BF16)

    attn_tasks = [_cross_attn_tasks(xbs[k], k * SUB_TILE, w_in_ref, kt_ref, v_ref, cat_ref, zx_ref)
                  for k in range(n_sub)]
    n_steps = n_sub * N_GROUPS
    z_next = _mix_proj(xbs[0], w_in_ref, 0, (0, 1, 2, 3))
    for s in range(n_steps):
        k, g = divmod(s, N_GROUPS)
        z = z_next
        if s + 1 < n_steps:
            k1, g1 = divmod(s + 1, N_GROUPS)
            z_next = _mix_proj(xbs[k1], w_in_ref, g1, (0, 1, 2, 3))
        if attn_tasks[k]:
            attn_tasks[k].pop(0)()
        mix(k, g, z)
        if g == N_GROUPS - 1:
            while attn_tasks[k]:
                attn_tasks[k].pop(0)()
            _out_proj_norm(k * SUB_TILE, x_ref, cat_ref, w_out_ref, g_ref, b_ref, o_ref)


def _layer_call(body, name, x, w_in, w_out, g, b, kt, v, *, layer, tile, extra, extra_scratch):
    bsz, slen, _ = x.shape
    const2 = lambda i, j: (0, 0)
    weights = lambda i, j: (layer, 0, 0)
    single = pl.Buffered(1)
    in_specs = [
        pl.BlockSpec((1, tile, D_MODEL), lambda i, j: (i, j, 0)),
        pl.BlockSpec((None, D_MODEL, D_IN), weights, pipeline_mode=single),
        pl.BlockSpec((None, D_CAT, D_MODEL), weights, pipeline_mode=single),
        pl.BlockSpec((1, D_MODEL), const2),
        pl.BlockSpec((1, D_MODEL), const2),
        pl.BlockSpec((1, D_XA, N_MEM), lambda i, j: (i, 0, 0)),
        pl.BlockSpec((1, N_MEM, D_XA), lambda i, j: (i, 0, 0)),
    ] + [pl.BlockSpec(e.shape, const2) for e in extra]
    return pl.pallas_call(
        body,
        grid=(bsz, slen // tile),
        in_specs=in_specs,
        out_specs=pl.BlockSpec((1, tile, D_MODEL), lambda i, j: (i, j, 0)),
        out_shape=jax.ShapeDtypeStruct((bsz, slen, D_MODEL), F32),
        scratch_shapes=[pltpu.VMEM((tile, D_CAT), BF16), pltpu.VMEM((tile, D_XA), BF16)] + extra_scratch,
        compiler_params=pltpu.CompilerParams(
            dimension_semantics=("arbitrary", "arbitrary"),
            vmem_limit_bytes=VMEM_LIMIT_BYTES),
        name=name,
    )(x, w_in, w_out, g, b, kt, v, *extra)


def kernel(x, mem, w_in, w_out, ln_g, ln_b, hgrn_lb_logits, hgrn_norm_w, conv_w,
           mem_ln_g, mem_ln_b, w_mem_kv):
    bsz, slen, d_model = x.shape
    assert d_model == D_MODEL and slen % HGRN_TILE == 0 and slen % CONV_TILE == 0 and bsz % MEM_BATCH == 0
    assert w_in.shape == (DEPTH, D_MODEL, D_IN) and w_out.shape == (DEPTH, D_CAT, D_MODEL)
    assert mem.shape == (bsz, N_MEM, D_MODEL)

    kt, v = _mem_kv(mem, w_mem_kv.astype(BF16), mem_ln_g.reshape(1, -1), mem_ln_b.reshape(1, -1))
    w_in_b = w_in.astype(BF16)
    w_out_b = w_out.astype(BF16)

    for layer in range(DEPTH):
        j = layer // 2
        g = ln_g[layer].reshape(1, -1)
        b = ln_b[layer].reshape(1, -1)
        args = (x, w_in_b, w_out_b, g, b, kt, v)
        if layer % 2 == 0:
            extra = [hgrn_lb_logits, hgrn_norm_w[j].reshape(1, -1)]
            state = pltpu.VMEM((HG_HEADS, HG_DK, HG_DK), F32)
            fast = functools.partial(
                _layer_call, functools.partial(_hgrn_layer_kernel, lb_index=j), "hgrn_layer",
                layer=layer, tile=HGRN_TILE, extra=extra, extra_scratch=[state] + [
                    pltpu.VMEM((SUB_TILE, 4 * GROUP), F32) for _ in range(HG_HEADS)])
            safe = functools.partial(
                _layer_call, functools.partial(_hgrn_safe_layer_kernel, lb_index=j), "hgrn_safe_layer",
                layer=layer, tile=SUB_TILE, extra=extra,
                extra_scratch=[state, pltpu.VMEM((SUB_TILE, 4 * D_MIX), F32)])
            log_lb = jax.nn.log_softmax(hgrn_lb_logits.astype(F32), axis=0)
            log_lb = jax.nn.logsumexp(log_lb[:j + 1], axis=0)
            fast_ok = 0.5 * HG_CHUNK * jnp.max(-log_lb) <= MAX_EXPONENT
            x = lax.cond(fast_ok, fast, safe, *args)
        else:
            x = _layer_call(_conv_layer_kernel, "conv_layer", *args, layer=layer, tile=CONV_TILE,
                            extra=[conv_w[j]], extra_scratch=[pltpu.VMEM((CONV_TILE + SUBLANES, D_MIX), F32)])
    return x
```

```python
import functools
import math

import jax
import jax.numpy as jnp
from jax import lax
from jax.experimental import pallas as pl
from jax.experimental.pallas import tpu as pltpu

D_MODEL = 1024
N_MEM = 256
HG_HEADS = 8
HG_DK = 128
D_MIX = 1024
CONV_WIDTH = 3
XA_HEADS = 4
XA_DH = 128
D_XA = XA_HEADS * XA_DH
D_IN = 4 * D_MIX + D_XA
D_CAT = D_MIX + D_XA
DEPTH = 2
DN_ALPHA = (2 * DEPTH) ** 0.25
LN_EPS = 1e-5
RMS_EPS = 1e-5

SUB_TILE = 512
HGRN_TILE = 2 * SUB_TILE
CONV_TILE = SUB_TILE
PROJ_LEAD = 8
MEM_BATCH = 4
HG_CHUNK = 128
GROUP = HG_DK
N_GROUPS = D_MIX // GROUP
OUT_ROW_SPLIT = 2
SUBLANES = 8
SAFE_BLOCK = SUBLANES
MAX_EXPONENT = 80.0
VMEM_LIMIT_BYTES = 58 * 1024 * 1024

F32 = jnp.float32
BF16 = jnp.bfloat16
LOG2E = math.log2(math.e)


def _layer_norm_rows(v, g, b):
    mu = jnp.mean(v, axis=-1, keepdims=True)
    d = v - mu
    var = jnp.mean(d * d, axis=-1, keepdims=True)
    return d * lax.rsqrt(var + LN_EPS) * g + b


def _sigmoid(v):
    return 1.0 / (1.0 + jnp.exp(-v))


def _dot(a, b):
    return jnp.dot(a, b, preferred_element_type=F32)


def _dot_nt(a, b):
    return lax.dot_general(a, b, (((1,), (1,)), ((), ())), preferred_element_type=F32)


def _dot_tn(a, b):
    return lax.dot_general(a, b, (((0,), (0,)), ((), ())), preferred_element_type=F32)


def _mem_kv_kernel(mem_ref, w_ref, g_ref, b_ref, kt_ref, v_ref):
    m = _layer_norm_rows(mem_ref[...].reshape(MEM_BATCH * N_MEM, D_MODEL), g_ref[...], b_ref[...])
    kv = _dot(m.astype(BF16), w_ref[...])
    for i in range(MEM_BATCH):
        rows = slice(i * N_MEM, (i + 1) * N_MEM)
        kt_ref[i] = kv[rows, :D_XA].T.astype(BF16)
        v_ref[i] = kv[rows, D_XA:].astype(BF16)


def _mem_kv(mem, w_mem_kv, g, b):
    bsz = mem.shape[0]
    return pl.pallas_call(
        _mem_kv_kernel,
        grid=(bsz // MEM_BATCH,),
        in_specs=[
            pl.BlockSpec((MEM_BATCH, N_MEM, D_MODEL), lambda i: (i, 0, 0)),
            pl.BlockSpec((D_MODEL, 2 * D_XA), lambda i: (0, 0)),
            pl.BlockSpec((1, D_MODEL), lambda i: (0, 0)),
            pl.BlockSpec((1, D_MODEL), lambda i: (0, 0)),
        ],
        out_specs=[
            pl.BlockSpec((MEM_BATCH, D_XA, N_MEM), lambda i: (i, 0, 0)),
            pl.BlockSpec((MEM_BATCH, N_MEM, D_XA), lambda i: (i, 0, 0)),
        ],
        out_shape=[
            jax.ShapeDtypeStruct((bsz, D_XA, N_MEM), BF16),
            jax.ShapeDtypeStruct((bsz, N_MEM, D_XA), BF16),
        ],
        compiler_params=pltpu.CompilerParams(
            dimension_semantics=("arbitrary",), vmem_limit_bytes=VMEM_LIMIT_BYTES),
        name="mem_kv",
    )(mem, w_mem_kv, g, b)


def _sub_rows(r0):
    return slice(r0, r0 + SUB_TILE)


def _mix_proj(xb, w_in_ref, g, parts):
    w = [w_in_ref[:, p * D_MIX + g * GROUP:p * D_MIX + (g + 1) * GROUP] for p in parts]
    return _dot(xb, jnp.concatenate(w, axis=1))


def _cumsum_rows(x):
    sub = SUBLANES
    row = lax.broadcasted_iota(jnp.int32, (sub, x.shape[1]), 0)
    out, carry = [], None
    for i in range(x.shape[0] // sub):
        blk = x[i * sub:(i + 1) * sub]
        for sh in (1, 2, 4):
            blk = blk + jnp.where(row >= sh, pltpu.roll(blk, sh, 0), 0.0)
        if carry is not None:
            blk = blk + carry
        carry = blk[sub - 1:sub, :]
        out.append(blk)
    return jnp.concatenate(out, axis=0)


def _rms_norm_gate(o, zg, nw):
    dv = o.shape[-1]
    r = lax.rsqrt(jnp.sum(o * o, axis=-1, keepdims=True) + dv * RMS_EPS)
    return o * r * (nw * dv ** 0.5) * (zg * _sigmoid(zg))


def _forget_lower_bound(lbl_ref, lb_index):
    lg = lbl_ref[...]
    lg = jnp.exp(lg - jnp.max(lg, axis=0, keepdims=True))
    return jnp.sum(lg[:lb_index + 1], axis=0, keepdims=True) / jnp.sum(lg, axis=0, keepdims=True)


def _cross_attn_tasks(xb, r0, w_in_ref, kt_ref, v_ref, cat_ref, zx_ref):
    c = XA_DH ** -0.5 * LOG2E
    rows = _sub_rows(r0)
    scores = {}

    def queries(n):
        cols = slice(n * 2 * XA_DH, (n + 1) * 2 * XA_DH)
        zx_ref[rows, cols] = _dot(
            xb, w_in_ref[:, 4 * D_MIX + cols.start:4 * D_MIX + cols.stop]).astype(BF16)

    def logits(h):
        hs = slice(h * XA_DH, (h + 1) * XA_DH)
        scores[h] = _dot(zx_ref[rows, hs], kt_ref[0, hs, :])

    def read(h):
        hs = slice(h * XA_DH, (h + 1) * XA_DH)
        s = scores.pop(h)
        e = jnp.exp2((s - jnp.max(s, axis=-1, keepdims=True)) * c)
        l = jnp.sum(e, axis=-1, keepdims=True)
        o = _dot(e.astype(BF16), v_ref[0, :, hs]) * (1.0 / l)
        cat_ref[rows, D_MIX + h * XA_DH:D_MIX + (h + 1) * XA_DH] = o.astype(BF16)

    tasks = [functools.partial(queries, 0), functools.partial(queries, 1)]
    for h in range(XA_HEADS):
        tasks += [functools.partial(logits, h), functools.partial(read, h)]
    return tasks


def _out_proj_norm(r0, x_ref, cat_ref, w_out_ref, g_ref, b_ref, o_ref):
    t = SUB_TILE // OUT_ROW_SPLIT
    slabs = [slice(r0 + r * t, r0 + (r + 1) * t) for r in range(OUT_ROW_SPLIT)]
    ys = [_dot(cat_ref[rows, :], w_out_ref[...]) for rows in slabs]
    for rows, y in zip(slabs, ys):
        o_ref[0, rows, :] = _layer_norm_rows(DN_ALPHA * x_ref[0, rows, :] + y, g_ref[...], b_ref[...])


def _hgrn_layer_kernel(x_ref, w_in_ref, w_out_ref, g_ref, b_ref, kt_ref, v_ref,
                       lbl_ref, nw_ref, o_ref, cat_ref, zx_ref, st_ref, *z_refs, lb_index):
    @pl.when(pl.program_id(1) == 0)
    def _():
        st_ref[...] = jnp.zeros_like(st_ref)

    n_sub = x_ref.shape[1] // SUB_TILE
    xbs = [x_ref[0, _sub_rows(k * SUB_TILE), :].astype(BF16) for k in range(n_sub)]
    lb_all = _forget_lower_bound(lbl_ref, lb_index)
    nw = nw_ref[...]

    c_len = HG_CHUNK
    n_chunks = SUB_TILE // c_len
    sub_items = HG_HEADS * n_chunks
    n_items = n_sub * sub_items
    row = lax.broadcasted_iota(jnp.int32, (c_len, c_len), 0)
    col = lax.broadcasted_iota(jnp.int32, (c_len, c_len), 1)
    causal = col <= row
    half = 2 * GROUP

    def item(i):
        k, rem = divmod(i, sub_items)
        h, ci = divmod(rem, n_chunks)
        zrows = slice(ci * c_len, (ci + 1) * c_len)
        return h, ci, zrows, slice(k * SUB_TILE + zrows.start, k * SUB_TILE + zrows.stop)

    def proj_piece(q):
        k, rem = divmod(q, 2 * HG_HEADS)
        h, n = divmod(rem, 2)
        z_refs[h][:, n * half:(n + 1) * half] = _mix_proj(xbs[k], w_in_ref, h, (2 * n, 2 * n + 1))

    def stage_a(i):
        h, _, zrows, _ = item(i)
        lb = lb_all[:, h * HG_DK:(h + 1) * HG_DK]
        zf = z_refs[h][zrows, HG_DK:2 * HG_DK]
        f = lb + (1.0 - lb) * _sigmoid(zf)
        return dict(kk=1.0 - f, bc=_cumsum_rows(jnp.log(f)))

    def stage_b(i, v):
        h, _, zrows, _ = item(i)
        zq = z_refs[h][zrows, 0:HG_DK]
        zi = z_refs[h][zrows, 2 * HG_DK:3 * HG_DK]
        bc = v["bc"]
        b_last = bc[c_len - 1:c_len, :]
        mid = 0.5 * b_last
        e_mid = jnp.exp(mid)
        qh = zq * _sigmoid(zq) * jnp.exp(bc - mid)
        kh = v["kk"] * jnp.exp(mid - bc)
        q_in = (qh * e_mid).astype(BF16)
        k_dec = (kh * e_mid).astype(BF16)
        vt = zi.T.astype(BF16)
        return dict(q_in=q_in, vt=vt, decay=jnp.exp(b_last),
                    scores=_dot_nt(qh.astype(BF16), kh.astype(BF16)), kv=_dot(vt, k_dec))

    def stage_c(v, st):
        a = jnp.where(causal, v["scores"], 0.0).astype(BF16)
        o = _dot_nt(jnp.concatenate([v["q_in"], a], axis=1),
                    jnp.concatenate([st.astype(BF16), v["vt"]], axis=1))
        return o, st * v["decay"] + v["kv"]

    def stage_d(i, o):
        h, _, zrows, rows = item(i)
        zg = z_refs[h][zrows, 3 * HG_DK:4 * HG_DK]
        cat_ref[rows, h * HG_DK:(h + 1) * HG_DK] = _rms_norm_gate(o, zg, nw).astype(BF16)

    n_pieces = n_sub * 2 * HG_HEADS
    attn_tasks = [_cross_attn_tasks(xbs[k], k * SUB_TILE, w_in_ref, kt_ref, v_ref, cat_ref, zx_ref)
                  for k in range(n_sub)]
    emitted = 0
    while 2 * emitted - PROJ_LEAD < -3:
        proj_piece(emitted)
        emitted += 1
    va, vb, vo, st = {}, {}, {}, None
    for s in range(-3, n_items):
        if emitted < n_pieces and 2 * emitted - PROJ_LEAD <= s:
            proj_piece(emitted)
            emitted += 1
        elif s >= 0 and attn_tasks[s // sub_items]:
            attn_tasks[s // sub_items].pop(0)()
        if 0 <= s + 3 < n_items:
            va[s + 3] = stage_a(s + 3)
        if 0 <= s + 2 < n_items:
            vb[s + 2] = stage_b(s + 2, va.pop(s + 2))
        if 0 <= s + 1 < n_items:
            h, ci, _, _ = item(s + 1)
            if ci == 0:
                st = st_ref[h]
            vo[s + 1], st = stage_c(vb.pop(s + 1), st)
            if ci == n_chunks - 1:
                st_ref[h] = st
        if 0 <= s:
            stage_d(s, vo.pop(s))
            k, rem = divmod(s + 1, sub_items)
            if rem == 0:
                while attn_tasks[k - 1]:
                    attn_tasks[k - 1].pop(0)()
    for k in range(n_sub):
        _out_proj_norm(k * SUB_TILE, x_ref, cat_ref, w_out_ref, g_ref, b_ref, o_ref)


def _hgrn_safe_layer_kernel(x_ref, w_in_ref, w_out_ref, g_ref, b_ref, kt_ref, v_ref,
                            lbl_ref, nw_ref, o_ref, cat_ref, zx_ref, st_ref, z_ref, *, lb_index):
    @pl.when(pl.program_id(1) == 0)
    def _():
        st_ref[...] = jnp.zeros_like(st_ref)

    xb = x_ref[0].astype(BF16)
    for p in range(4):
        for g in range(0, N_GROUPS, 2):
            cols = slice(p * D_MIX + g * GROUP, p * D_MIX + (g + 2) * GROUP)
            z_ref[:, cols] = _dot(xb, w_in_ref[:, cols])
    for task in _cross_attn_tasks(xb, 0, w_in_ref, kt_ref, v_ref, cat_ref, zx_ref):
        task()

    lb_all = _forget_lower_bound(lbl_ref, lb_index)
    nw = nw_ref[...]
    n = SAFE_BLOCK
    row = lax.broadcasted_iota(jnp.int32, (n, HG_DK), 0)
    shifts = [1 << j for j in range(n.bit_length() - 1)]

    def block(rows, h):
        hs = slice(h * HG_DK, (h + 1) * HG_DK)
        zq = z_ref[rows, h * HG_DK:(h + 1) * HG_DK]
        zf = z_ref[rows, D_MIX + h * HG_DK:D_MIX + (h + 1) * HG_DK]
        v = z_ref[rows, 2 * D_MIX + h * HG_DK:2 * D_MIX + (h + 1) * HG_DK]
        zg = z_ref[rows, 3 * D_MIX + h * HG_DK:3 * D_MIX + (h + 1) * HG_DK]
        lb = lb_all[:, hs]
        q = zq * _sigmoid(zq)
        f = lb + (1.0 - lb) * _sigmoid(zf)
        k = 1.0 - f
        pre = f
        suf = jnp.where(row < n - 1, pltpu.roll(f, n - 1, 0), 1.0)
        for sh in shifts:
            pre = pre * jnp.where(row >= sh, pltpu.roll(pre, sh, 0), 1.0)
            suf = suf * jnp.where(row + sh < n, pltpu.roll(suf, n - sh, 0), 1.0)
        st = st_ref[h]
        o = _dot_nt(q * pre, st)
        dec = None
        for lag in range(n):
            dec = jnp.ones_like(f) if lag == 0 else dec * pltpu.roll(f, lag - 1, 0)
            k_s = k if lag == 0 else pltpu.roll(k, lag, 0)
            v_s = v if lag == 0 else pltpu.roll(v, lag, 0)
            a = jnp.sum(q * k_s * dec, axis=-1, keepdims=True)
            o = o + jnp.where(row >= lag, a * v_s, 0.0)
        st_ref[h] = st * pre[n - 1:n, :] + _dot_tn(v, k * suf)
        return _rms_norm_gate(o, zg, nw)

    def block_pair(bi, carry):
        r0 = pl.multiple_of(bi * 2 * n, 2 * n)
        for h in range(HG_HEADS):
            o = [block(pl.ds(r0 + j * n, n), h) for j in range(2)]
            cat_ref[pl.ds(r0, 2 * n), h * HG_DK:(h + 1) * HG_DK] = jnp.concatenate(o, axis=0).astype(BF16)
        return carry

    lax.fori_loop(0, SUB_TILE // (2 * n), block_pair, 0)
    _out_proj_norm(0, x_ref, cat_ref, w_out_ref, g_ref, b_ref, o_ref)


def _conv_layer_kernel(x_ref, w_in_ref, w_out_ref, g_ref, b_ref, kt_ref, v_ref,
                       cw_ref, o_ref, cat_ref, zx_ref, u_ref):
    @pl.when(pl.program_id(1) == 0)
    def _():
        u_ref[0:SUBLANES, :] = jnp.zeros((SUBLANES, D_MIX), F32)

    tile = x_ref.shape[1]
    n_sub = tile // SUB_TILE
    xbs = [x_ref[0, _sub_rows(k * SUB_TILE), :].astype(BF16) for k in range(n_sub)]
    cw = cw_ref[...]

    def mix(k, g, z):
        gs = slice(g * GROUP, (g + 1) * GROUP)
        r0 = SUBLANES + k * SUB_TILE
        u = z[:, GROUP:2 * GROUP] * z[:, 2 * GROUP:3 * GROUP]
        u_ref[r0:r0 + SUB_TILE, gs] = u
        conv = cw[CONV_WIDTH - 1:CONV_WIDTH, gs] * u
        for back in range(1, CONV_WIDTH):
            tap = CONV_WIDTH - 1 - back
            conv = conv + cw[tap:tap + 1, gs] * u_ref[r0 - back:r0 - back + SUB_TILE, gs]
        if k == n_sub - 1:
            u_ref[0:SUBLANES, gs] = u_ref[tile:tile + SUBLANES, gs]
        zg = z[:, 3 * GROUP:4 * GROUP]
        cat_ref[_sub_rows(k * SUB_TILE), gs] = (z[:, 0:GROUP] * conv * (zg * _sigmoid(zg))).astype(BF16)

    attn_tasks = [_cross_attn_tasks(xbs[k], k * SUB_TILE, w_in_ref, kt_ref, v_ref, cat_ref, zx_ref)
                  for k in range(n_sub)]
    n_steps = n_sub * N_GROUPS
    z_next = _mix_proj(xbs[0], w_in_ref, 0, (0, 1, 2, 3))
    for s in range(n_steps):
        k, g = divmod(s, N_GROUPS)
        z = z_next
        if s + 1 < n_steps:
            k1, g1 = divmod(s + 1, N_GROUPS)
            z_next = _mix_proj(xbs[k1], w_in_ref, g1, (0, 1, 2, 3))
        if attn_tasks[k]:
            attn_tasks[k].pop(0)()
        mix(k, g, z)
        if g == N_GROUPS - 1:
            while attn_tasks[k]:
                attn_tasks[k].pop(0)()
            _out_proj_norm(k * SUB_TILE, x_ref, cat_ref, w_out_ref, g_ref, b_ref, o_ref)


def _layer_call(body, name, x, w_in, w_out, g, b, kt, v, *, layer, tile, extra, extra_scratch):
    bsz, slen, _ = x.shape
    const2 = lambda i, j: (0, 0)
    weights = lambda i, j: (layer, 0, 0)
    single = pl.Buffered(1)
    in_specs = [
        pl.BlockSpec((1, tile, D_MODEL), lambda i, j: (i, j, 0)),
        pl.BlockSpec((None, D_MODEL, D_IN), weights, pipeline_mode=single),
        pl.BlockSpec((None, D_CAT, D_MODEL), weights, pipeline_mode=single),
        pl.BlockSpec((1, D_MODEL), const2),
        pl.BlockSpec((1, D_MODEL), const2),
        pl.BlockSpec((1, D_XA, N_MEM), lambda i, j: (i, 0, 0)),
        pl.BlockSpec((1, N_MEM, D_XA), lambda i, j: (i, 0, 0)),
    ] + [pl.BlockSpec(e.shape, const2) for e in extra]
    return pl.pallas_call(
        body,
        grid=(bsz, slen // tile),
        in_specs=in_specs,
        out_specs=pl.BlockSpec((1, tile, D_MODEL), lambda i, j: (i, j, 0)),
        out_shape=jax.ShapeDtypeStruct((bsz, slen, D_MODEL), F32),
        scratch_shapes=[pltpu.VMEM((tile, D_CAT), BF16), pltpu.VMEM((tile, D_XA), BF16)] + extra_scratch,
        compiler_params=pltpu.CompilerParams(
            dimension_semantics=("arbitrary", "arbitrary"),
            vmem_limit_bytes=VMEM_LIMIT_BYTES),
        name=name,
    )(x, w_in, w_out, g, b, kt, v, *extra)


def kernel(x, mem, w_in, w_out, ln_g, ln_b, hgrn_lb_logits, hgrn_norm_w, conv_w,
           mem_ln_g, mem_ln_b, w_mem_kv):
    bsz, slen, d_model = x.shape
    assert d_model == D_MODEL and slen % HGRN_TILE == 0 and slen % CONV_TILE == 0 and bsz % MEM_BATCH == 0
    assert w_in.shape == (DEPTH, D_MODEL, D_IN) and w_out.shape == (DEPTH, D_CAT, D_MODEL)
    assert mem.shape == (bsz, N_MEM, D_MODEL)

    kt, v = _mem_kv(mem, w_mem_kv.astype(BF16), mem_ln_g.reshape(1, -1), mem_ln_b.reshape(1, -1))
    w_in_b = w_in.astype(BF16)
    w_out_b = w_out.astype(BF16)

    for layer in range(DEPTH):
        j = layer // 2
        g = ln_g[layer].reshape(1, -1)
        b = ln_b[layer].reshape(1, -1)
        args = (x, w_in_b, w_out_b, g, b, kt, v)
        if layer % 2 == 0:
            extra = [hgrn_lb_logits, hgrn_norm_w[j].reshape(1, -1)]
            state = pltpu.VMEM((HG_HEADS, HG_DK, HG_DK), F32)
            fast = functools.partial(
                _layer_call, functools.partial(_hgrn_layer_kernel, lb_index=j), "hgrn_layer",
                layer=layer, tile=HGRN_TILE, extra=extra, extra_scratch=[state] + [
                    pltpu.VMEM((SUB_TILE, 4 * GROUP), F32) for _ in range(HG_HEADS)])
            safe = functools.partial(
                _layer_call, functools.partial(_hgrn_safe_layer_kernel, lb_index=j), "hgrn_safe_layer",
                layer=layer, tile=SUB_TILE, extra=extra,
                extra_scratch=[state, pltpu.VMEM((SUB_TILE, 4 * D_MIX), F32)])
            log_lb = jax.nn.log_softmax(hgrn_lb_logits.astype(F32), axis=0)
            log_lb = jax.nn.logsumexp(log_lb[:j + 1], axis=0)
            fast_ok = 0.5 * HG_CHUNK * jnp.max(-log_lb) <= MAX_EXPONENT
            x = lax.cond(fast_ok, fast, safe, *args)
        else:
            x = _layer_call(_conv_layer_kernel, "conv_layer", *args, layer=layer, tile=CONV_TILE,
                            extra=[conv_w[j]], extra_scratch=[pltpu.VMEM((CONV_TILE + SUBLANES, D_MIX), F32)])
    return x
```

```python
import functools
import math

import jax
import jax.numpy as jnp
from jax import lax
from jax.experimental import pallas as pl
from jax.experimental.pallas import tpu as pltpu

D_MODEL = 1024
N_MEM = 256
HG_HEADS = 8
HG_DK = 128
D_MIX = 1024
CONV_WIDTH = 3
XA_HEADS = 4
XA_DH = 128
D_XA = XA_HEADS * XA_DH
D_IN = 4 * D_MIX + D_XA
D_CAT = D_MIX + D_XA
DEPTH = 2
DN_ALPHA = (2 * DEPTH) ** 0.25
LN_EPS = 1e-5
RMS_EPS = 1e-5

SUB_TILE = 512
HGRN_TILE = 2 * SUB_TILE
CONV_TILE = 2 * SUB_TILE
PROJ_LEAD = 8
MEM_BATCH = 4
HG_CHUNK = 128
GROUP = HG_DK
N_GROUPS = D_MIX // GROUP
OUT_ROW_SPLIT = 2
SUBLANES = 8
SAFE_BLOCK = SUBLANES
MAX_EXPONENT = 80.0
VMEM_LIMIT_BYTES = 58 * 1024 * 1024

F32 = jnp.float32
BF16 = jnp.bfloat16
LOG2E = math.log2(math.e)


def _layer_norm_rows(v, g, b):
    mu = jnp.mean(v, axis=-1, keepdims=True)
    d = v - mu
    var = jnp.mean(d * d, axis=-1, keepdims=True)
    return d * lax.rsqrt(var + LN_EPS) * g + b


def _sigmoid(v):
    return 1.0 / (1.0 + jnp.exp(-v))


def _dot(a, b):
    return jnp.dot(a, b, preferred_element_type=F32)


def _dot_nt(a, b):
    return lax.dot_general(a, b, (((1,), (1,)), ((), ())), preferred_element_type=F32)


def _dot_tn(a, b):
    return lax.dot_general(a, b, (((0,), (0,)), ((), ())), preferred_element_type=F32)


def _mem_kv_kernel(mem_ref, w_ref, g_ref, b_ref, kt_ref, v_ref):
    m = _layer_norm_rows(mem_ref[...].reshape(MEM_BATCH * N_MEM, D_MODEL), g_ref[...], b_ref[...])
    kv = _dot(m.astype(BF16), w_ref[...])
    for i in range(MEM_BATCH):
        rows = slice(i * N_MEM, (i + 1) * N_MEM)
        kt_ref[i] = kv[rows, :D_XA].T.astype(BF16)
        v_ref[i] = kv[rows, D_XA:].astype(BF16)


def _mem_kv(mem, w_mem_kv, g, b):
    bsz = mem.shape[0]
    return pl.pallas_call(
        _mem_kv_kernel,
        grid=(bsz // MEM_BATCH,),
        in_specs=[
            pl.BlockSpec((MEM_BATCH, N_MEM, D_MODEL), lambda i: (i, 0, 0)),
            pl.BlockSpec((D_MODEL, 2 * D_XA), lambda i: (0, 0)),
            pl.BlockSpec((1, D_MODEL), lambda i: (0, 0)),
            pl.BlockSpec((1, D_MODEL), lambda i: (0, 0)),
        ],
        out_specs=[
            pl.BlockSpec((MEM_BATCH, D_XA, N_MEM), lambda i: (i, 0, 0)),
            pl.BlockSpec((MEM_BATCH, N_MEM, D_XA), lambda i: (i, 0, 0)),
        ],
        out_shape=[
            jax.ShapeDtypeStruct((bsz, D_XA, N_MEM), BF16),
            jax.ShapeDtypeStruct((bsz, N_MEM, D_XA), BF16),
        ],
        compiler_params=pltpu.CompilerParams(
            dimension_semantics=("arbitrary",), vmem_limit_bytes=VMEM_LIMIT_BYTES),
        name="mem_kv",
    )(mem, w_mem_kv, g, b)


def _sub_rows(r0):
    return slice(r0, r0 + SUB_TILE)


def _mix_proj(xb, w_in_ref, g, parts):
    w = [w_in_ref[:, p * D_MIX + g * GROUP:p * D_MIX + (g + 1) * GROUP] for p in parts]
    return _dot(xb, jnp.concatenate(w, axis=1))


def _cumsum_rows(x):
    sub = SUBLANES
    row = lax.broadcasted_iota(jnp.int32, (sub, x.shape[1]), 0)
    out, carry = [], None
    for i in range(x.shape[0] // sub):
        blk = x[i * sub:(i + 1) * sub]
        for sh in (1, 2, 4):
            blk = blk + jnp.where(row >= sh, pltpu.roll(blk, sh, 0), 0.0)
        if carry is not None:
            blk = blk + carry
        carry = blk[sub - 1:sub, :]
        out.append(blk)
    return jnp.concatenate(out, axis=0)


def _rms_norm_gate(o, zg, nw):
    dv = o.shape[-1]
    r = lax.rsqrt(jnp.sum(o * o, axis=-1, keepdims=True) + dv * RMS_EPS)
    return o * r * (nw * dv ** 0.5) * (zg * _sigmoid(zg))


def _forget_lower_bound(lbl_ref, lb_index):
    lg = lbl_ref[...]
    lg = jnp.exp(lg - jnp.max(lg, axis=0, keepdims=True))
    return jnp.sum(lg[:lb_index + 1], axis=0, keepdims=True) / jnp.sum(lg, axis=0, keepdims=True)


def _cross_attn_tasks(xb, r0, w_in_ref, kt_ref, v_ref, cat_ref, zx_ref):
    c = XA_DH ** -0.5 * LOG2E
    rows = _sub_rows(r0)
    scores = {}

    def queries(n):
        cols = slice(n * 2 * XA_DH, (n + 1) * 2 * XA_DH)
        zx_ref[rows, cols] = _dot(
            xb, w_in_ref[:, 4 * D_MIX + cols.start:4 * D_MIX + cols.stop]).astype(BF16)

    def logits(h):
        hs = slice(h * XA_DH, (h + 1) * XA_DH)
        scores[h] = _dot(zx_ref[rows, hs], kt_ref[0, hs, :])

    def read(h):
        hs = slice(h * XA_DH, (h + 1) * XA_DH)
        s = scores.pop(h)
        e = jnp.exp2((s - jnp.max(s, axis=-1, keepdims=True)) * c)
        l = jnp.sum(e, axis=-1, keepdims=True)
        o = _dot(e.astype(BF16), v_ref[0, :, hs]) * (1.0 / l)
        cat_ref[rows, D_MIX + h * XA_DH:D_MIX + (h + 1) * XA_DH] = o.astype(BF16)

    tasks = [functools.partial(queries, 0), functools.partial(queries, 1)]
    for h in range(XA_HEADS):
        tasks += [functools.partial(logits, h), functools.partial(read, h)]
    return tasks


def _out_proj_norm(r0, x_ref, cat_ref, w_out_ref, g_ref, b_ref, o_ref):
    t = SUB_TILE // OUT_ROW_SPLIT
    slabs = [slice(r0 + r * t, r0 + (r + 1) * t) for r in range(OUT_ROW_SPLIT)]
    ys = [_dot(cat_ref[rows, :], w_out_ref[...]) for rows in slabs]
    for rows, y in zip(slabs, ys):
        o_ref[0, rows, :] = _layer_norm_rows(DN_ALPHA * x_ref[0, rows, :] + y, g_ref[...], b_ref[...])


def _hgrn_layer_kernel(x_ref, w_in_ref, w_out_ref, g_ref, b_ref, kt_ref, v_ref,
                       lbl_ref, nw_ref, o_ref, cat_ref, zx_ref, st_ref, *z_refs, lb_index):
    @pl.when(pl.program_id(1) == 0)
    def _():
        st_ref[...] = jnp.zeros_like(st_ref)

    n_sub = x_ref.shape[1] // SUB_TILE
    xbs = [x_ref[0, _sub_rows(k * SUB_TILE), :].astype(BF16) for k in range(n_sub)]
    lb_all = _forget_lower_bound(lbl_ref, lb_index)
    nw = nw_ref[...]

    c_len = HG_CHUNK
    n_chunks = SUB_TILE // c_len
    sub_items = HG_HEADS * n_chunks
    n_items = n_sub * sub_items
    row = lax.broadcasted_iota(jnp.int32, (c_len, c_len), 0)
    col = lax.broadcasted_iota(jnp.int32, (c_len, c_len), 1)
    causal = col <= row
    half = 2 * GROUP

    def item(i):
        k, rem = divmod(i, sub_items)
        h, ci = divmod(rem, n_chunks)
        zrows = slice(ci * c_len, (ci + 1) * c_len)
        return h, ci, zrows, slice(k * SUB_TILE + zrows.start, k * SUB_TILE + zrows.stop)

    def proj_piece(q):
        k, rem = divmod(q, 2 * HG_HEADS)
        h, n = divmod(rem, 2)
        z_refs[h][:, n * half:(n + 1) * half] = _mix_proj(xbs[k], w_in_ref, h, (2 * n, 2 * n + 1))

    def stage_a(i):
        h, _, zrows, _ = item(i)
        lb = lb_all[:, h * HG_DK:(h + 1) * HG_DK]
        zf = z_refs[h][zrows, HG_DK:2 * HG_DK]
        f = lb + (1.0 - lb) * _sigmoid(zf)
        return dict(kk=1.0 - f, bc=_cumsum_rows(jnp.log(f)))

    def stage_b(i, v):
        h, _, zrows, _ = item(i)
        zq = z_refs[h][zrows, 0:HG_DK]
        zi = z_refs[h][zrows, 2 * HG_DK:3 * HG_DK]
        bc = v["bc"]
        b_last = bc[c_len - 1:c_len, :]
        mid = 0.5 * b_last
        e_mid = jnp.exp(mid)
        qh = zq * _sigmoid(zq) * jnp.exp(bc - mid)
        kh = v["kk"] * jnp.exp(mid - bc)
        q_in = (qh * e_mid).astype(BF16)
        k_dec = (kh * e_mid).astype(BF16)
        vt = zi.T.astype(BF16)
        return dict(q_in=q_in, vt=vt, decay=jnp.exp(b_last),
                    scores=_dot_nt(qh.astype(BF16), kh.astype(BF16)), kv=_dot(vt, k_dec))

    def stage_c(v, st):
        a = jnp.where(causal, v["scores"], 0.0).astype(BF16)
        o = _dot_nt(jnp.concatenate([v["q_in"], a], axis=1),
                    jnp.concatenate([st.astype(BF16), v["vt"]], axis=1))
        return o, st * v["decay"] + v["kv"]

    def stage_d(i, o):
        h, _, zrows, rows = item(i)
        zg = z_refs[h][zrows, 3 * HG_DK:4 * HG_DK]
        cat_ref[rows, h * HG_DK:(h + 1) * HG_DK] = _rms_norm_gate(o, zg, nw).astype(BF16)

    n_pieces = n_sub * 2 * HG_HEADS
    attn_tasks = [_cross_attn_tasks(xbs[k], k * SUB_TILE, w_in_ref, kt_ref, v_ref, cat_ref, zx_ref)
                  for k in range(n_sub)]
    emitted = 0
    while 2 * emitted - PROJ_LEAD < -3:
        proj_piece(emitted)
        emitted += 1
    va, vb, vo, st = {}, {}, {}, None
    for s in range(-3, n_items):
        if emitted < n_pieces and 2 * emitted - PROJ_LEAD <= s:
            proj_piece(emitted)
            emitted += 1
        elif s >= 0 and attn_tasks[s // sub_items]:
            attn_tasks[s // sub_items].pop(0)()
        if 0 <= s + 3 < n_items:
            va[s + 3] = stage_a(s + 3)
        if 0 <= s + 2 < n_items:
            vb[s + 2] = stage_b(s + 2, va.pop(s + 2))
        if 0 <= s + 1 < n_items:
            h, ci, _, _ = item(s + 1)
            if ci == 0:
                st = st_ref[h]
            vo[s + 1], st = stage_c(vb.pop(s + 1), st)
            if ci == n_chunks - 1:
                st_ref[h] = st
        if 0 <= s:
            stage_d(s, vo.pop(s))
            k, rem = divmod(s + 1, sub_items)
            if rem == 0:
                while attn_tasks[k - 1]:
                    attn_tasks[k - 1].pop(0)()
    for k in range(n_sub):
        _out_proj_norm(k * SUB_TILE, x_ref, cat_ref, w_out_ref, g_ref, b_ref, o_ref)


def _hgrn_safe_layer_kernel(x_ref, w_in_ref, w_out_ref, g_ref, b_ref, kt_ref, v_ref,
                            lbl_ref, nw_ref, o_ref, cat_ref, zx_ref, st_ref, z_ref, *, lb_index):
    @pl.when(pl.program_id(1) == 0)
    def _():
        st_ref[...] = jnp.zeros_like(st_ref)

    xb = x_ref[0].astype(BF16)
    for p in range(4):
        for g in range(0, N_GROUPS, 2):
            cols = slice(p * D_MIX + g * GROUP, p * D_MIX + (g + 2) * GROUP)
            z_ref[:, cols] = _dot(xb, w_in_ref[:, cols])
    for task in _cross_attn_tasks(xb, 0, w_in_ref, kt_ref, v_ref, cat_ref, zx_ref):
        task()

    lb_all = _forget_lower_bound(lbl_ref, lb_index)
    nw = nw_ref[...]
    n = SAFE_BLOCK
    row = lax.broadcasted_iota(jnp.int32, (n, HG_DK), 0)
    shifts = [1 << j for j in range(n.bit_length() - 1)]

    def block(rows, h):
        hs = slice(h * HG_DK, (h + 1) * HG_DK)
        zq = z_ref[rows, h * HG_DK:(h + 1) * HG_DK]
        zf = z_ref[rows, D_MIX + h * HG_DK:D_MIX + (h + 1) * HG_DK]
        v = z_ref[rows, 2 * D_MIX + h * HG_DK:2 * D_MIX + (h + 1) * HG_DK]
        zg = z_ref[rows, 3 * D_MIX + h * HG_DK:3 * D_MIX + (h + 1) * HG_DK]
        lb = lb_all[:, hs]
        q = zq * _sigmoid(zq)
        f = lb + (1.0 - lb) * _sigmoid(zf)
        k = 1.0 - f
        pre = f
        suf = jnp.where(row < n - 1, pltpu.roll(f, n - 1, 0), 1.0)
        for sh in shifts:
            pre = pre * jnp.where(row >= sh, pltpu.roll(pre, sh, 0), 1.0)
            suf = suf * jnp.where(row + sh < n, pltpu.roll(suf, n - sh, 0), 1.0)
        st = st_ref[h]
        o = _dot_nt(q * pre, st)
        dec = None
        for lag in range(n):
            dec = jnp.ones_like(f) if lag == 0 else dec * pltpu.roll(f, lag - 1, 0)
            k_s = k if lag == 0 else pltpu.roll(k, lag, 0)
            v_s = v if lag == 0 else pltpu.roll(v, lag, 0)
            a = jnp.sum(q * k_s * dec, axis=-1, keepdims=True)
            o = o + jnp.where(row >= lag, a * v_s, 0.0)
        st_ref[h] = st * pre[n - 1:n, :] + _dot_tn(v, k * suf)
        return _rms_norm_gate(o, zg, nw)

    def block_pair(bi, carry):
        r0 = pl.multiple_of(bi * 2 * n, 2 * n)
        for h in range(HG_HEADS):
            o = [block(pl.ds(r0 + j * n, n), h) for j in range(2)]
            cat_ref[pl.ds(r0, 2 * n), h * HG_DK:(h + 1) * HG_DK] = jnp.concatenate(o, axis=0).astype(BF16)
        return carry

    lax.fori_loop(0, SUB_TILE // (2 * n), block_pair, 0)
    _out_proj_norm(0, x_ref, cat_ref, w_out_ref, g_ref, b_ref, o_ref)


def _conv_layer_kernel(x_ref, w_in_ref, w_out_ref, g_ref, b_ref, kt_ref, v_ref,
                       cw_ref, o_ref, cat_ref, zx_ref, u_ref):
    @pl.when(pl.program_id(1) == 0)
    def _():
        u_ref[0:SUBLANES, :] = jnp.zeros((SUBLANES, D_MIX), F32)

    tile = x_ref.shape[1]
    n_sub = tile // SUB_TILE
    xbs = [x_ref[0, _sub_rows(k * SUB_TILE), :].astype(BF16) for k in range(n_sub)]
    cw = cw_ref[...]

    def mix(k, g, z):
        gs = slice(g * GROUP, (g + 1) * GROUP)
        r0 = SUBLANES + k * SUB_TILE
        u = z[:, GROUP:2 * GROUP] * z[:, 2 * GROUP:3 * GROUP]
        u_ref[r0:r0 + SUB_TILE, gs] = u
        conv = cw[CONV_WIDTH - 1:CONV_WIDTH, gs] * u
        for back in range(1, CONV_WIDTH):
            tap = CONV_WIDTH - 1 - back
            conv = conv + cw[tap:tap + 1, gs] * u_ref[r0 - back:r0 - back + SUB_TILE, gs]
        if k == n_sub - 1:
            u_ref[0:SUBLANES, gs] = u_ref[tile:tile + SUBLANES, gs]
        zg = z[:, 3 * GROUP:4 * GROUP]
        cat_ref[_sub_rows(k * SUB_TILE), gs] = (z[:, 0:GROUP] * conv * (zg * _sigmoid(zg))).astype(BF16)

    attn_tasks = [_cross_attn_tasks(xbs[k], k * SUB_TILE, w_in_ref, kt_ref, v_ref, cat_ref, zx_ref)
                  for k in range(n_sub)]
    n_steps = n_sub * N_GROUPS
    z_next = _mix_proj(xbs[0], w_in_ref, 0, (0, 1, 2, 3))
    for s in range(n_steps):
        k, g = divmod(s, N_GROUPS)
        z = z_next
        if s + 1 < n_steps:
            k1, g1 = divmod(s + 1, N_GROUPS)
            z_next = _mix_proj(xbs[k1], w_in_ref, g1, (0, 1, 2, 3))
        if attn_tasks[k]:
            attn_tasks[k].pop(0)()
        mix(k, g, z)
        if g == N_GROUPS - 1:
            while attn_tasks[k]:
                attn_tasks[k].pop(0)()
    for k in range(n_sub):
        _out_proj_norm(k * SUB_TILE, x_ref, cat_ref, w_out_ref, g_ref, b_ref, o_ref)


def _layer_call(body, name, x, w_in, w_out, g, b, kt, v, *, layer, tile, extra, extra_scratch):
    bsz, slen, _ = x.shape
    const2 = lambda i, j: (0, 0)
    weights = lambda i, j: (layer, 0, 0)
    single = pl.Buffered(1)
    in_specs = [
        pl.BlockSpec((1, tile, D_MODEL), lambda i, j: (i, j, 0)),
        pl.BlockSpec((None, D_MODEL, D_IN), weights, pipeline_mode=single),
        pl.BlockSpec((None, D_CAT, D_MODEL), weights, pipeline_mode=single),
        pl.BlockSpec((1, D_MODEL), const2),
        pl.BlockSpec((1, D_MODEL), const2),
        pl.BlockSpec((1, D_XA, N_MEM), lambda i, j: (i, 0, 0)),
        pl.BlockSpec((1, N_MEM, D_XA), lambda i, j: (i, 0, 0)),
    ] + [pl.BlockSpec(e.shape, const2) for e in extra]
    return pl.pallas_call(
        body,
        grid=(bsz, slen // tile),
        in_specs=in_specs,
        out_specs=pl.BlockSpec((1, tile, D_MODEL), lambda i, j: (i, j, 0)),
        out_shape=jax.ShapeDtypeStruct((bsz, slen, D_MODEL), F32),
        scratch_shapes=[pltpu.VMEM((tile, D_CAT), BF16), pltpu.VMEM((tile, D_XA), BF16)] + extra_scratch,
        compiler_params=pltpu.CompilerParams(
            dimension_semantics=("arbitrary", "arbitrary"),
            vmem_limit_bytes=VMEM_LIMIT_BYTES),
        name=name,
    )(x, w_in, w_out, g, b, kt, v, *extra)


def kernel(x, mem, w_in, w_out, ln_g, ln_b, hgrn_lb_logits, hgrn_norm_w, conv_w,
           mem_ln_g, mem_ln_b, w_mem_kv):
    bsz, slen, d_model = x.shape
    assert d_model == D_MODEL and slen % HGRN_TILE == 0 and slen % CONV_TILE == 0 and bsz % MEM_BATCH == 0
    assert w_in.shape == (DEPTH, D_MODEL, D_IN) and w_out.shape == (DEPTH, D_CAT, D_MODEL)
    assert mem.shape == (bsz, N_MEM, D_MODEL)

    kt, v = _mem_kv(mem, w_mem_kv.astype(BF16), mem_ln_g.reshape(1, -1), mem_ln_b.reshape(1, -1))
    w_in_b = w_in.astype(BF16)
    w_out_b = w_out.astype(BF16)

    for layer in range(DEPTH):
        j = layer // 2
        g = ln_g[layer].reshape(1, -1)
        b = ln_b[layer].reshape(1, -1)
        args = (x, w_in_b, w_out_b, g, b, kt, v)
        if layer % 2 == 0:
            extra = [hgrn_lb_logits, hgrn_norm_w[j].reshape(1, -1)]
            state = pltpu.VMEM((HG_HEADS, HG_DK, HG_DK), F32)
            fast = functools.partial(
                _layer_call, functools.partial(_hgrn_layer_kernel, lb_index=j), "hgrn_layer",
                layer=layer, tile=HGRN_TILE, extra=extra, extra_scratch=[state] + [
                    pltpu.VMEM((SUB_TILE, 4 * GROUP), F32) for _ in range(HG_HEADS)])
            safe = functools.partial(
                _layer_call, functools.partial(_hgrn_safe_layer_kernel, lb_index=j), "hgrn_safe_layer",
                layer=layer, tile=SUB_TILE, extra=extra,
                extra_scratch=[state, pltpu.VMEM((SUB_TILE, 4 * D_MIX), F32)])
            log_lb = jax.nn.log_softmax(hgrn_lb_logits.astype(F32), axis=0)
            log_lb = jax.nn.logsumexp(log_lb[:j + 1], axis=0)
            fast_ok = 0.5 * HG_CHUNK * jnp.max(-log_lb) <= MAX_EXPONENT
            x = lax.cond(fast_ok, fast, safe, *args)
        else:
            x = _layer_call(_conv_layer_kernel, "conv_layer", *args, layer=layer, tile=CONV_TILE,
                            extra=[conv_w[j]], extra_scratch=[pltpu.VMEM((CONV_TILE + SUBLANES, D_MIX), F32)])
    return x
```

```python
import functools
import math

import jax
import jax.numpy as jnp
from jax import lax
from jax.experimental import pallas as pl
from jax.experimental.pallas import tpu as pltpu

D_MODEL = 1024
N_MEM = 256
HG_HEADS = 8
HG_DK = 128
D_MIX = 1024
CONV_WIDTH = 3
XA_HEADS = 4
XA_DH = 128
D_XA = XA_HEADS * XA_DH
D_IN = 4 * D_MIX + D_XA
D_CAT = D_MIX + D_XA
DEPTH = 2
DN_ALPHA = (2 * DEPTH) ** 0.25
LN_EPS = 1e-5
RMS_EPS = 1e-5

SUB_TILE = 512
HGRN_TILE = 2 * SUB_TILE
CONV_TILE = 2 * SUB_TILE
PROJ_LEAD = 8
MEM_BATCH = 4
HG_CHUNK = 128
GROUP = HG_DK
N_GROUPS = D_MIX // GROUP
OUT_ROW_SPLIT = 2
SUBLANES = 8
SAFE_BLOCK = SUBLANES
MAX_EXPONENT = 80.0
VMEM_LIMIT_BYTES = 58 * 1024 * 1024

F32 = jnp.float32
BF16 = jnp.bfloat16
LOG2E = math.log2(math.e)


def _layer_norm_rows(v, g, b):
    mu = jnp.mean(v, axis=-1, keepdims=True)
    d = v - mu
    var = jnp.mean(d * d, axis=-1, keepdims=True)
    return d * lax.rsqrt(var + LN_EPS) * g + b


def _sigmoid(v):
    return 1.0 / (1.0 + jnp.exp(-v))


def _dot(a, b):
    return jnp.dot(a, b, preferred_element_type=F32)


def _dot_nt(a, b):
    return lax.dot_general(a, b, (((1,), (1,)), ((), ())), preferred_element_type=F32)


def _dot_tn(a, b):
    return lax.dot_general(a, b, (((0,), (0,)), ((), ())), preferred_element_type=F32)


def _mem_kv_kernel(mem_ref, w_ref, g_ref, b_ref, kt_ref, v_ref):
    m = _layer_norm_rows(mem_ref[...].reshape(MEM_BATCH * N_MEM, D_MODEL), g_ref[...], b_ref[...])
    kv = _dot(m.astype(BF16), w_ref[...])
    for i in range(MEM_BATCH):
        rows = slice(i * N_MEM, (i + 1) * N_MEM)
        kt_ref[i] = kv[rows, :D_XA].T.astype(BF16)
        v_ref[i] = kv[rows, D_XA:].astype(BF16)


def _mem_kv(mem, w_mem_kv, g, b):
    bsz = mem.shape[0]
    return pl.pallas_call(
        _mem_kv_kernel,
        grid=(bsz // MEM_BATCH,),
        in_specs=[
            pl.BlockSpec((MEM_BATCH, N_MEM, D_MODEL), lambda i: (i, 0, 0)),
            pl.BlockSpec((D_MODEL, 2 * D_XA), lambda i: (0, 0)),
            pl.BlockSpec((1, D_MODEL), lambda i: (0, 0)),
            pl.BlockSpec((1, D_MODEL), lambda i: (0, 0)),
        ],
        out_specs=[
            pl.BlockSpec((MEM_BATCH, D_XA, N_MEM), lambda i: (i, 0, 0)),
            pl.BlockSpec((MEM_BATCH, N_MEM, D_XA), lambda i: (i, 0, 0)),
        ],
        out_shape=[
            jax.ShapeDtypeStruct((bsz, D_XA, N_MEM), BF16),
            jax.ShapeDtypeStruct((bsz, N_MEM, D_XA), BF16),
        ],
        compiler_params=pltpu.CompilerParams(
            dimension_semantics=("arbitrary",), vmem_limit_bytes=VMEM_LIMIT_BYTES),
        name="mem_kv",
    )(mem, w_mem_kv, g, b)


def _sub_rows(r0):
    return slice(r0, r0 + SUB_TILE)


def _mix_proj(xb, w_in_ref, g, parts):
    w = [w_in_ref[:, p * D_MIX + g * GROUP:p * D_MIX + (g + 1) * GROUP] for p in parts]
    return _dot(xb, jnp.concatenate(w, axis=1))


def _cumsum_rows(x):
    sub = SUBLANES
    row = lax.broadcasted_iota(jnp.int32, (sub, x.shape[1]), 0)
    out, carry = [], None
    for i in range(x.shape[0] // sub):
        blk = x[i * sub:(i + 1) * sub]
        for sh in (1, 2, 4):
            blk = blk + jnp.where(row >= sh, pltpu.roll(blk, sh, 0), 0.0)
        if carry is not None:
            blk = blk + carry
        carry = blk[sub - 1:sub, :]
        out.append(blk)
    return jnp.concatenate(out, axis=0)


def _rms_norm_gate(o, zg, nw):
    dv = o.shape[-1]
    r = lax.rsqrt(jnp.sum(o * o, axis=-1, keepdims=True) + dv * RMS_EPS)
    return o * r * (nw * dv ** 0.5) * (zg * _sigmoid(zg))


def _zero_anchor(y):
    bits = pltpu.bitcast(y, jnp.int32)
    acc = bits[0:SUBLANES]
    for r in range(SUBLANES, y.shape[0], SUBLANES):
        acc = acc | bits[r:r + SUBLANES]
    return lax.shift_right_logical(lax.shift_right_logical(acc, 16), 16).astype(F32)


def _forget_lower_bound(lbl_ref, lb_index):
    lg = lbl_ref[...]
    lg = jnp.exp(lg - jnp.max(lg, axis=0, keepdims=True))
    return jnp.sum(lg[:lb_index + 1], axis=0, keepdims=True) / jnp.sum(lg, axis=0, keepdims=True)


def _cross_attn_tasks(xb, r0, w_in_ref, kt_ref, v_ref, cat_ref, zx_ref):
    c = XA_DH ** -0.5 * LOG2E
    rows = _sub_rows(r0)
    scores = {}

    def queries(n):
        cols = slice(n * 2 * XA_DH, (n + 1) * 2 * XA_DH)
        zx_ref[rows, cols] = _dot(
            xb, w_in_ref[:, 4 * D_MIX + cols.start:4 * D_MIX + cols.stop]).astype(BF16)

    def logits(h):
        hs = slice(h * XA_DH, (h + 1) * XA_DH)
        scores[h] = _dot(zx_ref[rows, hs], kt_ref[0, hs, :])

    def read(h):
        hs = slice(h * XA_DH, (h + 1) * XA_DH)
        s = scores.pop(h)
        e = jnp.exp2((s - jnp.max(s, axis=-1, keepdims=True)) * c)
        l = jnp.sum(e, axis=-1, keepdims=True)
        o = _dot(e.astype(BF16), v_ref[0, :, hs]) * (1.0 / l)
        cat_ref[rows, D_MIX + h * XA_DH:D_MIX + (h + 1) * XA_DH] = o.astype(BF16)

    tasks = [functools.partial(queries, 0), functools.partial(queries, 1)]
    for h in range(XA_HEADS):
        tasks += [functools.partial(logits, h), functools.partial(read, h)]
    return tasks


def _out_proj_norm(r0, x_ref, cat_ref, w_out_ref, g_ref, b_ref, o_ref):
    t = SUB_TILE // OUT_ROW_SPLIT
    slabs = [slice(r0 + r * t, r0 + (r + 1) * t) for r in range(OUT_ROW_SPLIT)]
    ys = [_dot(cat_ref[rows, :], w_out_ref[...]) for rows in slabs]
    for rows, y in zip(slabs, ys):
        o_ref[0, rows, :] = _layer_norm_rows(DN_ALPHA * x_ref[0, rows, :] + y, g_ref[...], b_ref[...])


def _hgrn_layer_kernel(x_ref, w_in_ref, w_out_ref, g_ref, b_ref, kt_ref, v_ref,
                       lbl_ref, nw_ref, o_ref, cat_ref, zx_ref, st_ref, *z_refs, lb_index):
    @pl.when(pl.program_id(1) == 0)
    def _():
        st_ref[...] = jnp.zeros_like(st_ref)

    n_sub = x_ref.shape[1] // SUB_TILE
    xbs = [x_ref[0, _sub_rows(k * SUB_TILE), :].astype(BF16) for k in range(n_sub)]
    lb_all = _forget_lower_bound(lbl_ref, lb_index)
    nw = nw_ref[...]

    c_len = HG_CHUNK
    n_chunks = SUB_TILE // c_len
    sub_items = HG_HEADS * n_chunks
    n_items = n_sub * sub_items
    row = lax.broadcasted_iota(jnp.int32, (c_len, c_len), 0)
    col = lax.broadcasted_iota(jnp.int32, (c_len, c_len), 1)
    causal = col <= row
    half = 2 * GROUP

    def item(i):
        k, rem = divmod(i, sub_items)
        h, ci = divmod(rem, n_chunks)
        zrows = slice(ci * c_len, (ci + 1) * c_len)
        return h, ci, zrows, slice(k * SUB_TILE + zrows.start, k * SUB_TILE + zrows.stop)

    def proj_piece(q):
        k, rem = divmod(q, 2 * HG_HEADS)
        h, n = divmod(rem, 2)
        z_refs[h][:, n * half:(n + 1) * half] = _mix_proj(xbs[k], w_in_ref, h, (2 * n, 2 * n + 1))

    def stage_a(i):
        h, _, zrows, _ = item(i)
        lb = lb_all[:, h * HG_DK:(h + 1) * HG_DK]
        zf = z_refs[h][zrows, HG_DK:2 * HG_DK]
        f = lb + (1.0 - lb) * _sigmoid(zf)
        return dict(kk=1.0 - f, bc=_cumsum_rows(jnp.log(f)))

    def stage_b(i, v):
        h, _, zrows, _ = item(i)
        zq = z_refs[h][zrows, 0:HG_DK]
        zi = z_refs[h][zrows, 2 * HG_DK:3 * HG_DK]
        bc = v["bc"]
        b_last = bc[c_len - 1:c_len, :]
        mid = 0.5 * b_last
        e_mid = jnp.exp(mid)
        qh = zq * _sigmoid(zq) * jnp.exp(bc - mid)
        kh = v["kk"] * jnp.exp(mid - bc)
        q_in = (qh * e_mid).astype(BF16)
        k_dec = (kh * e_mid).astype(BF16)
        vt = zi.T.astype(BF16)
        return dict(q_in=q_in, vt=vt, decay=jnp.exp(b_last),
                    scores=_dot_nt(qh.astype(BF16), kh.astype(BF16)), kv=_dot(vt, k_dec))

    def stage_c(v, st, anchor):
        a = jnp.where(causal, v["scores"], 0.0).astype(BF16)
        s_mm = st
        if anchor is not None:
            s_mm = jnp.concatenate([st[:SUBLANES] + anchor, st[SUBLANES:]], axis=0)
        o = _dot_nt(jnp.concatenate([v["q_in"], a], axis=1),
                    jnp.concatenate([s_mm.astype(BF16), v["vt"]], axis=1))
        return o, st * v["decay"] + v["kv"]

    def stage_d(i, o):
        h, _, zrows, rows = item(i)
        zg = z_refs[h][zrows, 3 * HG_DK:4 * HG_DK]
        y = _rms_norm_gate(o, zg, nw)
        cat_ref[rows, h * HG_DK:(h + 1) * HG_DK] = y.astype(BF16)
        return _zero_anchor(y)

    n_pieces = n_sub * 2 * HG_HEADS
    attn_tasks = [_cross_attn_tasks(xbs[k], k * SUB_TILE, w_in_ref, kt_ref, v_ref, cat_ref, zx_ref)
                  for k in range(n_sub)]
    emitted = 0
    while 2 * emitted - PROJ_LEAD < -3:
        proj_piece(emitted)
        emitted += 1
    va, vb, vo, st, anchor = {}, {}, {}, None, None
    for s in range(-3, n_items):
        if emitted < n_pieces and 2 * emitted - PROJ_LEAD <= s:
            proj_piece(emitted)
            emitted += 1
        elif s >= 0 and attn_tasks[s // sub_items]:
            attn_tasks[s // sub_items].pop(0)()
        if 0 <= s + 3 < n_items:
            va[s + 3] = stage_a(s + 3)
        if 0 <= s + 2 < n_items:
            vb[s + 2] = stage_b(s + 2, va.pop(s + 2))
        if 0 <= s + 1 < n_items:
            h, ci, _, _ = item(s + 1)
            if ci == 0:
                st = st_ref[h]
            vo[s + 1], st = stage_c(vb.pop(s + 1), st, anchor)
            if ci == n_chunks - 1:
                st_ref[h] = st
        if 0 <= s:
            anchor = stage_d(s, vo.pop(s))
            k, rem = divmod(s + 1, sub_items)
            if rem == 0:
                while attn_tasks[k - 1]:
                    attn_tasks[k - 1].pop(0)()
    for k in range(n_sub):
        _out_proj_norm(k * SUB_TILE, x_ref, cat_ref, w_out_ref, g_ref, b_ref, o_ref)


def _hgrn_safe_layer_kernel(x_ref, w_in_ref, w_out_ref, g_ref, b_ref, kt_ref, v_ref,
                            lbl_ref, nw_ref, o_ref, cat_ref, zx_ref, st_ref, z_ref, *, lb_index):
    @pl.when(pl.program_id(1) == 0)
    def _():
        st_ref[...] = jnp.zeros_like(st_ref)

    xb = x_ref[0].astype(BF16)
    for p in range(4):
        for g in range(0, N_GROUPS, 2):
            cols = slice(p * D_MIX + g * GROUP, p * D_MIX + (g + 2) * GROUP)
            z_ref[:, cols] = _dot(xb, w_in_ref[:, cols])
    for task in _cross_attn_tasks(xb, 0, w_in_ref, kt_ref, v_ref, cat_ref, zx_ref):
        task()

    lb_all = _forget_lower_bound(lbl_ref, lb_index)
    nw = nw_ref[...]
    n = SAFE_BLOCK
    row = lax.broadcasted_iota(jnp.int32, (n, HG_DK), 0)
    shifts = [1 << j for j in range(n.bit_length() - 1)]

    def block(rows, h):
        hs = slice(h * HG_DK, (h + 1) * HG_DK)
        zq = z_ref[rows, h * HG_DK:(h + 1) * HG_DK]
        zf = z_ref[rows, D_MIX + h * HG_DK:D_MIX + (h + 1) * HG_DK]
        v = z_ref[rows, 2 * D_MIX + h * HG_DK:2 * D_MIX + (h + 1) * HG_DK]
        zg = z_ref[rows, 3 * D_MIX + h * HG_DK:3 * D_MIX + (h + 1) * HG_DK]
        lb = lb_all[:, hs]
        q = zq * _sigmoid(zq)
        f = lb + (1.0 - lb) * _sigmoid(zf)
        k = 1.0 - f
        pre = f
        suf = jnp.where(row < n - 1, pltpu.roll(f, n - 1, 0), 1.0)
        for sh in shifts:
            pre = pre * jnp.where(row >= sh, pltpu.roll(pre, sh, 0), 1.0)
            suf = suf * jnp.where(row + sh < n, pltpu.roll(suf, n - sh, 0), 1.0)
        st = st_ref[h]
        o = _dot_nt(q * pre, st)
        dec = None
        for lag in range(n):
            dec = jnp.ones_like(f) if lag == 0 else dec * pltpu.roll(f, lag - 1, 0)
            k_s = k if lag == 0 else pltpu.roll(k, lag, 0)
            v_s = v if lag == 0 else pltpu.roll(v, lag, 0)
            a = jnp.sum(q * k_s * dec, axis=-1, keepdims=True)
            o = o + jnp.where(row >= lag, a * v_s, 0.0)
        st_ref[h] = st * pre[n - 1:n, :] + _dot_tn(v, k * suf)
        return _rms_norm_gate(o, zg, nw)

    def block_pair(bi, carry):
        r0 = pl.multiple_of(bi * 2 * n, 2 * n)
        for h in range(HG_HEADS):
            o = [block(pl.ds(r0 + j * n, n), h) for j in range(2)]
            cat_ref[pl.ds(r0, 2 * n), h * HG_DK:(h + 1) * HG_DK] = jnp.concatenate(o, axis=0).astype(BF16)
        return carry

    lax.fori_loop(0, SUB_TILE // (2 * n), block_pair, 0)
    _out_proj_norm(0, x_ref, cat_ref, w_out_ref, g_ref, b_ref, o_ref)


def _conv_layer_kernel(x_ref, w_in_ref, w_out_ref, g_ref, b_ref, kt_ref, v_ref,
                       cw_ref, o_ref, cat_ref, zx_ref, u_ref):
    @pl.when(pl.program_id(1) == 0)
    def _():
        u_ref[0:SUBLANES, :] = jnp.zeros((SUBLANES, D_MIX), F32)

    tile = x_ref.shape[1]
    n_sub = tile // SUB_TILE
    xbs = [x_ref[0, _sub_rows(k * SUB_TILE), :].astype(BF16) for k in range(n_sub)]
    cw = cw_ref[...]

    def mix(k, g, z):
        gs = slice(g * GROUP, (g + 1) * GROUP)
        r0 = SUBLANES + k * SUB_TILE
        u = z[:, GROUP:2 * GROUP] * z[:, 2 * GROUP:3 * GROUP]
        u_ref[r0:r0 + SUB_TILE, gs] = u
        conv = cw[CONV_WIDTH - 1:CONV_WIDTH, gs] * u
        for back in range(1, CONV_WIDTH):
            tap = CONV_WIDTH - 1 - back
            conv = conv + cw[tap:tap + 1, gs] * u_ref[r0 - back:r0 - back + SUB_TILE, gs]
        if k == n_sub - 1:
            u_ref[0:SUBLANES, gs] = u_ref[tile:tile + SUBLANES, gs]
        zg = z[:, 3 * GROUP:4 * GROUP]
        cat_ref[_sub_rows(k * SUB_TILE), gs] = (z[:, 0:GROUP] * conv * (zg * _sigmoid(zg))).astype(BF16)

    attn_tasks = [_cross_attn_tasks(xbs[k], k * SUB_TILE, w_in_ref, kt_ref, v_ref, cat_ref, zx_ref)
                  for k in range(n_sub)]
    n_steps = n_sub * N_GROUPS
    z_next = _mix_proj(xbs[0], w_in_ref, 0, (0, 1, 2, 3))
    for s in range(n_steps):
        k, g = divmod(s, N_GROUPS)
        z = z_next
        if s + 1 < n_steps:
            k1, g1 = divmod(s + 1, N_GROUPS)
            z_next = _mix_proj(xbs[k1], w_in_ref, g1, (0, 1, 2, 3))
        if attn_tasks[k]:
            attn_tasks[k].pop(0)()
        mix(k, g, z)
        if g == N_GROUPS - 1:
            while attn_tasks[k]:
                attn_tasks[k].pop(0)()
    for k in range(n_sub):
        _out_proj_norm(k * SUB_TILE, x_ref, cat_ref, w_out_ref, g_ref, b_ref, o_ref)


def _layer_call(body, name, x, w_in, w_out, g, b, kt, v, *, layer, tile, extra, extra_scratch):
    bsz, slen, _ = x.shape
    const2 = lambda i, j: (0, 0)
    weights = lambda i, j: (layer, 0, 0)
    single = pl.Buffered(1)
    in_specs = [
        pl.BlockSpec((1, tile, D_MODEL), lambda i, j: (i, j, 0)),
        pl.BlockSpec((None, D_MODEL, D_IN), weights, pipeline_mode=single),
        pl.BlockSpec((None, D_CAT, D_MODEL), weights, pipeline_mode=single),
        pl.BlockSpec((1, D_MODEL), const2),
        pl.BlockSpec((1, D_MODEL), const2),
        pl.BlockSpec((1, D_XA, N_MEM), lambda i, j: (i, 0, 0)),
        pl.BlockSpec((1, N_MEM, D_XA), lambda i, j: (i, 0, 0)),
    ] + [pl.BlockSpec(e.shape, const2) for e in extra]
    return pl.pallas_call(
        body,
        grid=(bsz, slen // tile),
        in_specs=in_specs,
        out_specs=pl.BlockSpec((1, tile, D_MODEL), lambda i, j: (i, j, 0)),
        out_shape=jax.ShapeDtypeStruct((bsz, slen, D_MODEL), F32),
        scratch_shapes=[pltpu.VMEM((tile, D_CAT), BF16), pltpu.VMEM((tile, D_XA), BF16)] + extra_scratch,
        compiler_params=pltpu.CompilerParams(
            dimension_semantics=("arbitrary", "arbitrary"),
            vmem_limit_bytes=VMEM_LIMIT_BYTES),
        name=name,
    )(x, w_in, w_out, g, b, kt, v, *extra)


def kernel(x, mem, w_in, w_out, ln_g, ln_b, hgrn_lb_logits, hgrn_norm_w, conv_w,
           mem_ln_g, mem_ln_b, w_mem_kv):
    bsz, slen, d_model = x.shape
    assert d_model == D_MODEL and slen % HGRN_TILE == 0 and slen % CONV_TILE == 0 and bsz % MEM_BATCH == 0
    assert w_in.shape == (DEPTH, D_MODEL, D_IN) and w_out.shape == (DEPTH, D_CAT, D_MODEL)
    assert mem.shape == (bsz, N_MEM, D_MODEL)

    kt, v = _mem_kv(mem, w_mem_kv.astype(BF16), mem_ln_g.reshape(1, -1), mem_ln_b.reshape(1, -1))
    w_in_b = w_in.astype(BF16)
    w_out_b = w_out.astype(BF16)

    for layer in range(DEPTH):
        j = layer // 2
        g = ln_g[layer].reshape(1, -1)
        b = ln_b[layer].reshape(1, -1)
        args = (x, w_in_b, w_out_b, g, b, kt, v)
        if layer % 2 == 0:
            extra = [hgrn_lb_logits, hgrn_norm_w[j].reshape(1, -1)]
            state = pltpu.VMEM((HG_HEADS, HG_DK, HG_DK), F32)
            fast = functools.partial(
                _layer_call, functools.partial(_hgrn_layer_kernel, lb_index=j), "hgrn_layer",
                layer=layer, tile=HGRN_TILE, extra=extra, extra_scratch=[state] + [
                    pltpu.VMEM((SUB_TILE, 4 * GROUP), F32) for _ in range(HG_HEADS)])
            safe = functools.partial(
                _layer_call, functools.partial(_hgrn_safe_layer_kernel, lb_index=j), "hgrn_safe_layer",
                layer=layer, tile=SUB_TILE, extra=extra,
                extra_scratch=[state, pltpu.VMEM((SUB_TILE, 4 * D_MIX), F32)])
            log_lb = jax.nn.log_softmax(hgrn_lb_logits.astype(F32), axis=0)
            log_lb = jax.nn.logsumexp(log_lb[:j + 1], axis=0)
            fast_ok = 0.5 * HG_CHUNK * jnp.max(-log_lb) <= MAX_EXPONENT
            x = lax.cond(fast_ok, fast, safe, *args)
        else:
            x = _layer_call(_conv_layer_kernel, "conv_layer", *args, layer=layer, tile=CONV_TILE,
                            extra=[conv_w[j]], extra_scratch=[pltpu.VMEM((CONV_TILE + SUBLANES, D_MIX), F32)])
    return x
```

```python
import functools
import math

import jax
import jax.numpy as jnp
from jax import lax
from jax.experimental import pallas as pl
from jax.experimental.pallas import tpu as pltpu

D_MODEL = 1024
N_MEM = 256
HG_HEADS = 8
HG_DK = 128
D_MIX = 1024
CONV_WIDTH = 3
XA_HEADS = 4
XA_DH = 128
D_XA = XA_HEADS * XA_DH
D_IN = 4 * D_MIX + D_XA
D_CAT = D_MIX + D_XA
DEPTH = 2
DN_ALPHA = (2 * DEPTH) ** 0.25
LN_EPS = 1e-5
RMS_EPS = 1e-5

SUB_TILE = 512
HGRN_TILE = 2 * SUB_TILE
CONV_TILE = 2 * SUB_TILE
PROJ_LEAD = 8
MEM_BATCH = 4
HG_CHUNK = 128
GROUP = HG_DK
N_GROUPS = D_MIX // GROUP
OUT_ROW_SPLIT = 2
SUBLANES = 8
SAFE_BLOCK = SUBLANES
MAX_EXPONENT = 80.0
VMEM_LIMIT_BYTES = 58 * 1024 * 1024

F32 = jnp.float32
BF16 = jnp.bfloat16
LOG2E = math.log2(math.e)


def _layer_norm_rows(v, g, b):
    mu = jnp.mean(v, axis=-1, keepdims=True)
    d = v - mu
    var = jnp.mean(d * d, axis=-1, keepdims=True)
    return d * lax.rsqrt(var + LN_EPS) * g + b


def _sigmoid(v):
    return 1.0 / (1.0 + jnp.exp(-v))


def _dot(a, b):
    return jnp.dot(a, b, preferred_element_type=F32)


def _dot_nt(a, b):
    return lax.dot_general(a, b, (((1,), (1,)), ((), ())), preferred_element_type=F32)


def _dot_tn(a, b):
    return lax.dot_general(a, b, (((0,), (0,)), ((), ())), preferred_element_type=F32)


def _mem_kv_kernel(mem_ref, w_ref, g_ref, b_ref, kt_ref, v_ref):
    m = _layer_norm_rows(mem_ref[...].reshape(MEM_BATCH * N_MEM, D_MODEL), g_ref[...], b_ref[...])
    kv = _dot(m.astype(BF16), w_ref[...])
    for i in range(MEM_BATCH):
        rows = slice(i * N_MEM, (i + 1) * N_MEM)
        kt_ref[i] = kv[rows, :D_XA].T.astype(BF16)
        v_ref[i] = kv[rows, D_XA:].astype(BF16)


def _mem_kv(mem, w_mem_kv, g, b):
    bsz = mem.shape[0]
    return pl.pallas_call(
        _mem_kv_kernel,
        grid=(bsz // MEM_BATCH,),
        in_specs=[
            pl.BlockSpec((MEM_BATCH, N_MEM, D_MODEL), lambda i: (i, 0, 0)),
            pl.BlockSpec((D_MODEL, 2 * D_XA), lambda i: (0, 0)),
            pl.BlockSpec((1, D_MODEL), lambda i: (0, 0)),
            pl.BlockSpec((1, D_MODEL), lambda i: (0, 0)),
        ],
        out_specs=[
            pl.BlockSpec((MEM_BATCH, D_XA, N_MEM), lambda i: (i, 0, 0)),
            pl.BlockSpec((MEM_BATCH, N_MEM, D_XA), lambda i: (i, 0, 0)),
        ],
        out_shape=[
            jax.ShapeDtypeStruct((bsz, D_XA, N_MEM), BF16),
            jax.ShapeDtypeStruct((bsz, N_MEM, D_XA), BF16),
        ],
        compiler_params=pltpu.CompilerParams(
            dimension_semantics=("arbitrary",), vmem_limit_bytes=VMEM_LIMIT_BYTES),
        name="mem_kv",
    )(mem, w_mem_kv, g, b)


def _sub_rows(r0):
    return slice(r0, r0 + SUB_TILE)


def _mix_proj(xb, w_in_ref, g, parts):
    w = [w_in_ref[:, p * D_MIX + g * GROUP:p * D_MIX + (g + 1) * GROUP] for p in parts]
    return _dot(xb, jnp.concatenate(w, axis=1))


def _cumsum_rows(x):
    sub = SUBLANES
    row = lax.broadcasted_iota(jnp.int32, (sub, x.shape[1]), 0)
    out, carry = [], None
    for i in range(x.shape[0] // sub):
        blk = x[i * sub:(i + 1) * sub]
        for sh in (1, 2, 4):
            blk = blk + jnp.where(row >= sh, pltpu.roll(blk, sh, 0), 0.0)
        if carry is not None:
            blk = blk + carry
        carry = blk[sub - 1:sub, :]
        out.append(blk)
    return jnp.concatenate(out, axis=0)


def _rms_norm_gate(o, zg, nw):
    dv = o.shape[-1]
    r = lax.rsqrt(jnp.sum(o * o, axis=-1, keepdims=True) + dv * RMS_EPS)
    return o * r * (nw * dv ** 0.5) * (zg * _sigmoid(zg))


def _forget_lower_bound(lbl_ref, lb_index):
    lg = lbl_ref[...]
    lg = jnp.exp(lg - jnp.max(lg, axis=0, keepdims=True))
    return jnp.sum(lg[:lb_index + 1], axis=0, keepdims=True) / jnp.sum(lg, axis=0, keepdims=True)


def _cross_attn_tasks(xb, r0, w_in_ref, kt_ref, v_ref, cat_ref, zx_ref):
    c = XA_DH ** -0.5 * LOG2E
    rows = _sub_rows(r0)
    scores = {}

    def queries(n):
        cols = slice(n * 2 * XA_DH, (n + 1) * 2 * XA_DH)
        zx_ref[rows, cols] = _dot(
            xb, w_in_ref[:, 4 * D_MIX + cols.start:4 * D_MIX + cols.stop]).astype(BF16)

    def logits(h):
        hs = slice(h * XA_DH, (h + 1) * XA_DH)
        scores[h] = _dot(zx_ref[rows, hs], kt_ref[0, hs, :])

    def read(h):
        hs = slice(h * XA_DH, (h + 1) * XA_DH)
        s = scores.pop(h)
        e = jnp.exp2((s - jnp.max(s, axis=-1, keepdims=True)) * c)
        l = jnp.sum(e, axis=-1, keepdims=True)
        o = _dot(e.astype(BF16), v_ref[0, :, hs]) * (1.0 / l)
        cat_ref[rows, D_MIX + h * XA_DH:D_MIX + (h + 1) * XA_DH] = o.astype(BF16)

    tasks = [functools.partial(queries, 0), functools.partial(queries, 1)]
    for h in range(XA_HEADS):
        tasks += [functools.partial(logits, h), functools.partial(read, h)]
    return tasks


def _out_proj_norm(r0, x_ref, cat_ref, w_out_ref, g_ref, b_ref, o_ref):
    t = SUB_TILE // OUT_ROW_SPLIT
    slabs = [slice(r0 + r * t, r0 + (r + 1) * t) for r in range(OUT_ROW_SPLIT)]
    ys = [_dot(cat_ref[rows, :], w_out_ref[...]) for rows in slabs]
    for rows, y in zip(slabs, ys):
        o_ref[0, rows, :] = _layer_norm_rows(DN_ALPHA * x_ref[0, rows, :] + y, g_ref[...], b_ref[...])


def _hgrn_layer_kernel(x_ref, w_in_ref, w_out_ref, g_ref, b_ref, kt_ref, v_ref,
                       lbl_ref, nw_ref, o_ref, cat_ref, zx_ref, st_ref, *z_refs, lb_index):
    @pl.when(pl.program_id(1) == 0)
    def _():
        st_ref[...] = jnp.zeros_like(st_ref)

    n_sub = x_ref.shape[1] // SUB_TILE
    xbs = [x_ref[0, _sub_rows(k * SUB_TILE), :].astype(BF16) for k in range(n_sub)]
    lb_all = _forget_lower_bound(lbl_ref, lb_index)
    nw = nw_ref[...]

    c_len = HG_CHUNK
    n_chunks = SUB_TILE // c_len
    sub_items = HG_HEADS * n_chunks
    n_items = n_sub * sub_items
    row = lax.broadcasted_iota(jnp.int32, (c_len, c_len), 0)
    col = lax.broadcasted_iota(jnp.int32, (c_len, c_len), 1)
    causal = col <= row
    half = 2 * GROUP

    def item(i):
        k, rem = divmod(i, sub_items)
        h, ci = divmod(rem, n_chunks)
        zrows = slice(ci * c_len, (ci + 1) * c_len)
        return h, ci, zrows, slice(k * SUB_TILE + zrows.start, k * SUB_TILE + zrows.stop)

    def proj_piece(q):
        k, rem = divmod(q, 2 * HG_HEADS)
        h, n = divmod(rem, 2)
        z_refs[h][:, n * half:(n + 1) * half] = _mix_proj(xbs[k], w_in_ref, h, (2 * n, 2 * n + 1))

    def stage_a(i):
        h, _, zrows, _ = item(i)
        lb = lb_all[:, h * HG_DK:(h + 1) * HG_DK]
        zf = z_refs[h][zrows, HG_DK:2 * HG_DK]
        f = lb + (1.0 - lb) * _sigmoid(zf)
        return dict(kk=1.0 - f, bc=_cumsum_rows(jnp.log(f)))

    def stage_b(i, v):
        h, _, zrows, _ = item(i)
        zq = z_refs[h][zrows, 0:HG_DK]
        zi = z_refs[h][zrows, 2 * HG_DK:3 * HG_DK]
        bc = v["bc"]
        b_last = bc[c_len - 1:c_len, :]
        mid = 0.5 * b_last
        e_mid = jnp.exp(mid)
        qh = zq * _sigmoid(zq) * jnp.exp(bc - mid)
        kh = v["kk"] * jnp.exp(mid - bc)
        q_in = (qh * e_mid).astype(BF16)
        k_dec = (kh * e_mid).astype(BF16)
        vt = zi.T.astype(BF16)
        return dict(q_in=q_in, vt=vt, decay=jnp.exp(b_last),
                    scores=_dot_nt(qh.astype(BF16), kh.astype(BF16)), kv=_dot(vt, k_dec))

    def stage_c(v, st):
        a = jnp.where(causal, v["scores"], 0.0).astype(BF16)
        o = _dot_nt(jnp.concatenate([v["q_in"], a], axis=1),
                    jnp.concatenate([st.astype(BF16), v["vt"]], axis=1))
        return o, st * v["decay"] + v["kv"]

    def stage_d(i, o):
        h, _, zrows, rows = item(i)
        zg = z_refs[h][zrows, 3 * HG_DK:4 * HG_DK]
        cat_ref[rows, h * HG_DK:(h + 1) * HG_DK] = _rms_norm_gate(o, zg, nw).astype(BF16)

    n_pieces = n_sub * 2 * HG_HEADS
    attn_tasks = [_cross_attn_tasks(xbs[k], k * SUB_TILE, w_in_ref, kt_ref, v_ref, cat_ref, zx_ref)
                  for k in range(n_sub)]
    emitted = 0
    while 2 * emitted - PROJ_LEAD < -3:
        proj_piece(emitted)
        emitted += 1
    va, vb, vo, st = {}, {}, {}, None
    for s in range(-3, n_items):
        if emitted < n_pieces and 2 * emitted - PROJ_LEAD <= s:
            proj_piece(emitted)
            emitted += 1
        elif s >= 0 and attn_tasks[s // sub_items]:
            attn_tasks[s // sub_items].pop(0)()
        if 0 <= s + 3 < n_items:
            va[s + 3] = stage_a(s + 3)
        if 0 <= s + 2 < n_items:
            vb[s + 2] = stage_b(s + 2, va.pop(s + 2))
        if 0 <= s + 1 < n_items:
            h, ci, _, _ = item(s + 1)
            if ci == 0:
                st = st_ref[h]
            vo[s + 1], st = stage_c(vb.pop(s + 1), st)
            if ci == n_chunks - 1:
                st_ref[h] = st
        if 0 <= s:
            stage_d(s, vo.pop(s))
            k, rem = divmod(s + 1, sub_items)
            if rem == 0:
                while attn_tasks[k - 1]:
                    attn_tasks[k - 1].pop(0)()
    for k in range(n_sub):
        _out_proj_norm(k * SUB_TILE, x_ref, cat_ref, w_out_ref, g_ref, b_ref, o_ref)


def _hgrn_safe_layer_kernel(x_ref, w_in_ref, w_out_ref, g_ref, b_ref, kt_ref, v_ref,
                            lbl_ref, nw_ref, o_ref, cat_ref, zx_ref, st_ref, z_ref, *, lb_index):
    @pl.when(pl.program_id(1) == 0)
    def _():
        st_ref[...] = jnp.zeros_like(st_ref)

    xb = x_ref[0].astype(BF16)
    for p in range(4):
        for g in range(0, N_GROUPS, 2):
            cols = slice(p * D_MIX + g * GROUP, p * D_MIX + (g + 2) * GROUP)
            z_ref[:, cols] = _dot(xb, w_in_ref[:, cols])
    for task in _cross_attn_tasks(xb, 0, w_in_ref, kt_ref, v_ref, cat_ref, zx_ref):
        task()

    lb_all = _forget_lower_bound(lbl_ref, lb_index)
    nw = nw_ref[...]
    n = SAFE_BLOCK
    row = lax.broadcasted_iota(jnp.int32, (n, HG_DK), 0)
    shifts = [1 << j for j in range(n.bit_length() - 1)]

    def block(rows, h):
        hs = slice(h * HG_DK, (h + 1) * HG_DK)
        zq = z_ref[rows, h * HG_DK:(h + 1) * HG_DK]
        zf = z_ref[rows, D_MIX + h * HG_DK:D_MIX + (h + 1) * HG_DK]
        v = z_ref[rows, 2 * D_MIX + h * HG_DK:2 * D_MIX + (h + 1) * HG_DK]
        zg = z_ref[rows, 3 * D_MIX + h * HG_DK:3 * D_MIX + (h + 1) * HG_DK]
        lb = lb_all[:, hs]
        q = zq * _sigmoid(zq)
        f = lb + (1.0 - lb) * _sigmoid(zf)
        k = 1.0 - f
        pre = f
        suf = jnp.where(row < n - 1, pltpu.roll(f, n - 1, 0), 1.0)
        for sh in shifts:
            pre = pre * jnp.where(row >= sh, pltpu.roll(pre, sh, 0), 1.0)
            suf = suf * jnp.where(row + sh < n, pltpu.roll(suf, n - sh, 0), 1.0)
        st = st_ref[h]
        o = _dot_nt(q * pre, st)
        dec = None
        for lag in range(n):
            dec = jnp.ones_like(f) if lag == 0 else dec * pltpu.roll(f, lag - 1, 0)
            k_s = k if lag == 0 else pltpu.roll(k, lag, 0)
            v_s = v if lag == 0 else pltpu.roll(v, lag, 0)
            a = jnp.sum(q * k_s * dec, axis=-1, keepdims=True)
            o = o + jnp.where(row >= lag, a * v_s, 0.0)
        st_ref[h] = st * pre[n - 1:n, :] + _dot_tn(v, k * suf)
        return _rms_norm_gate(o, zg, nw)

    def block_pair(bi, carry):
        r0 = pl.multiple_of(bi * 2 * n, 2 * n)
        for h in range(HG_HEADS):
            o = [block(pl.ds(r0 + j * n, n), h) for j in range(2)]
            cat_ref[pl.ds(r0, 2 * n), h * HG_DK:(h + 1) * HG_DK] = jnp.concatenate(o, axis=0).astype(BF16)
        return carry

    lax.fori_loop(0, SUB_TILE // (2 * n), block_pair, 0)
    _out_proj_norm(0, x_ref, cat_ref, w_out_ref, g_ref, b_ref, o_ref)


def _conv_layer_kernel(x_ref, w_in_ref, w_out_ref, g_ref, b_ref, kt_ref, v_ref,
                       cw_ref, o_ref, cat_ref, zx_ref, u_ref):
    @pl.when(pl.program_id(1) == 0)
    def _():
        u_ref[0:SUBLANES, :] = jnp.zeros((SUBLANES, D_MIX), F32)

    tile = x_ref.shape[1]
    n_sub = tile // SUB_TILE
    xbs = [x_ref[0, _sub_rows(k * SUB_TILE), :].astype(BF16) for k in range(n_sub)]
    cw = cw_ref[...]

    def mix(k, g, z):
        gs = slice(g * GROUP, (g + 1) * GROUP)
        r0 = SUBLANES + k * SUB_TILE
        u = z[:, GROUP:2 * GROUP] * z[:, 2 * GROUP:3 * GROUP]
        u_ref[r0:r0 + SUB_TILE, gs] = u
        conv = cw[CONV_WIDTH - 1:CONV_WIDTH, gs] * u
        for back in range(1, CONV_WIDTH):
            tap = CONV_WIDTH - 1 - back
            conv = conv + cw[tap:tap + 1, gs] * u_ref[r0 - back:r0 - back + SUB_TILE, gs]
        if k == n_sub - 1:
            u_ref[0:SUBLANES, gs] = u_ref[tile:tile + SUBLANES, gs]
        zg = z[:, 3 * GROUP:4 * GROUP]
        cat_ref[_sub_rows(k * SUB_TILE), gs] = (z[:, 0:GROUP] * conv * (zg * _sigmoid(zg))).astype(BF16)

    attn_tasks = [_cross_attn_tasks(xbs[k], k * SUB_TILE, w_in_ref, kt_ref, v_ref, cat_ref, zx_ref)
                  for k in range(n_sub)]
    n_steps = n_sub * N_GROUPS
    z_next = _mix_proj(xbs[0], w_in_ref, 0, (0, 1, 2, 3))
    for s in range(n_steps):
        k, g = divmod(s, N_GROUPS)
        z = z_next
        if s + 1 < n_steps:
            k1, g1 = divmod(s + 1, N_GROUPS)
            z_next = _mix_proj(xbs[k1], w_in_ref, g1, (0, 1, 2, 3))
        if attn_tasks[k]:
            attn_tasks[k].pop(0)()
        mix(k, g, z)
        if g == N_GROUPS - 1:
            while attn_tasks[k]:
                attn_tasks[k].pop(0)()
    for k in range(n_sub):
        _out_proj_norm(k * SUB_TILE, x_ref, cat_ref, w_out_ref, g_ref, b_ref, o_ref)


def _layer_call(body, name, x, w_in, w_out, g, b, kt, v, *, layer, tile, extra, extra_scratch):
    bsz, slen, _ = x.shape
    const2 = lambda i, j: (0, 0)
    weights = lambda i, j: (layer, 0, 0)
    single = pl.Buffered(1)
    in_specs = [
        pl.BlockSpec((1, tile, D_MODEL), lambda i, j: (i, j, 0)),
        pl.BlockSpec((None, D_MODEL, D_IN), weights, pipeline_mode=single),
        pl.BlockSpec((None, D_CAT, D_MODEL), weights, pipeline_mode=single),
        pl.BlockSpec((1, D_MODEL), const2),
        pl.BlockSpec((1, D_MODEL), const2),
        pl.BlockSpec((1, D_XA, N_MEM), lambda i, j: (i, 0, 0)),
        pl.BlockSpec((1, N_MEM, D_XA), lambda i, j: (i, 0, 0)),
    ] + [pl.BlockSpec(e.shape, const2) for e in extra]
    return pl.pallas_call(
        body,
        grid=(bsz, slen // tile),
        in_specs=in_specs,
        out_specs=pl.BlockSpec((1, tile, D_MODEL), lambda i, j: (i, j, 0)),
        out_shape=jax.ShapeDtypeStruct((bsz, slen, D_MODEL), F32),
        scratch_shapes=[pltpu.VMEM((tile, D_CAT), BF16), pltpu.VMEM((tile, D_XA), BF16)] + extra_scratch,
        compiler_params=pltpu.CompilerParams(
            dimension_semantics=("parallel", "arbitrary"),
            vmem_limit_bytes=VMEM_LIMIT_BYTES),
        name=name,
    )(x, w_in, w_out, g, b, kt, v, *extra)


def kernel(x, mem, w_in, w_out, ln_g, ln_b, hgrn_lb_logits, hgrn_norm_w, conv_w,
           mem_ln_g, mem_ln_b, w_mem_kv):
    bsz, slen, d_model = x.shape
    assert d_model == D_MODEL and slen % HGRN_TILE == 0 and slen % CONV_TILE == 0 and bsz % MEM_BATCH == 0
    assert w_in.shape == (DEPTH, D_MODEL, D_IN) and w_out.shape == (DEPTH, D_CAT, D_MODEL)
    assert mem.shape == (bsz, N_MEM, D_MODEL)

    kt, v = _mem_kv(mem, w_mem_kv.astype(BF16), mem_ln_g.reshape(1, -1), mem_ln_b.reshape(1, -1))
    w_in_b = w_in.astype(BF16)
    w_out_b = w_out.astype(BF16)

    for layer in range(DEPTH):
        j = layer // 2
        g = ln_g[layer].reshape(1, -1)
        b = ln_b[layer].reshape(1, -1)
        args = (x, w_in_b, w_out_b, g, b, kt, v)
        if layer % 2 == 0:
            extra = [hgrn_lb_logits, hgrn_norm_w[j].reshape(1, -1)]
            state = pltpu.VMEM((HG_HEADS, HG_DK, HG_DK), F32)
            fast = functools.partial(
                _layer_call, functools.partial(_hgrn_layer_kernel, lb_index=j), "hgrn_layer",
                layer=layer, tile=HGRN_TILE, extra=extra, extra_scratch=[state] + [
                    pltpu.VMEM((SUB_TILE, 4 * GROUP), F32) for _ in range(HG_HEADS)])
            safe = functools.partial(
                _layer_call, functools.partial(_hgrn_safe_layer_kernel, lb_index=j), "hgrn_safe_layer",
                layer=layer, tile=SUB_TILE, extra=extra,
                extra_scratch=[state, pltpu.VMEM((SUB_TILE, 4 * D_MIX), F32)])
            log_lb = jax.nn.log_softmax(hgrn_lb_logits.astype(F32), axis=0)
            log_lb = jax.nn.logsumexp(log_lb[:j + 1], axis=0)
            fast_ok = 0.5 * HG_CHUNK * jnp.max(-log_lb) <= MAX_EXPONENT
            x = lax.cond(fast_ok, fast, safe, *args)
        else:
            x = _layer_call(_conv_layer_kernel, "conv_layer", *args, layer=layer, tile=CONV_TILE,
                            extra=[conv_w[j]], extra_scratch=[pltpu.VMEM((CONV_TILE + SUBLANES, D_MIX), F32)])
    return x
```
